```python
import math
import jax, jax.numpy as jnp
from jax import lax
import numpy as np

D_MODEL = 1024
BATCH = 4
SEQ = 4096
DEPTH = 2

HEAD_DIM = 64
N_HEADS_TOTAL = D_MODEL // HEAD_DIM
N_SB_HEADS = N_HEADS_TOTAL // 4
DIL_GROUPS = ((128, 1), (512, 4), (2048, 16))
N_DIL_GROUPS = len(DIL_GROUPS)
HEADS_PER_GROUP = (N_HEADS_TOTAL - N_SB_HEADS) // N_DIL_GROUPS
N_DIL_HEADS = HEADS_PER_GROUP * N_DIL_GROUPS
D_DIL = N_DIL_HEADS * HEAD_DIM
D_DIL_OUT = HEADS_PER_GROUP * HEAD_DIM
D_SB = N_SB_HEADS * HEAD_DIM
D_IN = 3 * D_DIL + 3 * D_SB + 2 * D_MODEL
D_FF = 128 * ((8 * D_MODEL // 3 + 127) // 128)
ROPE_THETA = 500000.0
ROPE_DIM = HEAD_DIM // 4
Q_BLOCK = 128
RMS_EPS = 1e-6

kernel_name = "hybrid_dilated_stickbreaking_macaron"


def rms_norm(x, gain):
    xf = x.astype(jnp.float32)
    xf = xf * lax.rsqrt(jnp.mean(xf * xf, axis=-1, keepdims=True) + RMS_EPS)
    return (xf * gain.astype(jnp.float32)).astype(x.dtype)


def swiglu(x, w_gate, w_up, w_down):
    return (jax.nn.silu(x @ w_gate) * (x @ w_up)) @ w_down


def rope_tables(seq_len):
    pos = jnp.arange(seq_len, dtype=jnp.float32)
    inv_freq = ROPE_THETA ** (-jnp.arange(0, ROPE_DIM, 2, dtype=jnp.float32) / ROPE_DIM)
    ang = pos[:, None] * inv_freq[None, :]
    return jnp.cos(ang), jnp.sin(ang)


def apply_partial_rope(x, cos, sin):
    half = ROPE_DIM // 2
    x1 = x[..., :half]
    x2 = x[..., half:ROPE_DIM]
    c = cos.astype(x.dtype)
    s = sin.astype(x.dtype)
    return jnp.concatenate([x1 * c - x2 * s, x2 * c + x1 * s, x[..., ROPE_DIM:]], axis=-1)


def dilated_window_attention(q, k, v, window, dilation):
    B, H, T, dh = q.shape
    span = window // dilation
    unit = span * dilation
    t_pad = -(-T // unit) * unit
    n_sub = t_pad // dilation
    n_blk = n_sub // span

    def to_blocks(a):
        a = jnp.pad(a, ((0, 0), (0, 0), (0, t_pad - T), (0, 0)))
        a = a.reshape(B, H, n_sub, dilation, dh).transpose(0, 1, 3, 2, 4)
        return a.reshape(B, H, dilation, n_blk, span, dh)

    qb, kb, vb = to_blocks(q), to_blocks(k), to_blocks(v)

    def with_prev(a):
        prev = jnp.pad(a, ((0, 0), (0, 0), (0, 0), (1, 0), (0, 0), (0, 0)))[:, :, :, :-1]
        return jnp.concatenate([prev, a], axis=4)

    kw, vw = with_prev(kb), with_prev(vb)
    s = jnp.einsum('bhrnqd,bhrnkd->bhrnqk', qb, kw).astype(jnp.float32) * (dh ** -0.5)
    qi = jnp.arange(span)[:, None]
    kj = jnp.arange(2 * span)[None, :]
    dist = qi + span - kj
    band = (dist >= 0) & (dist <= span)
    blk = jnp.arange(n_blk)[:, None, None]
    valid = band[None] & ((blk > 0) | (kj >= span)[None])
    s = jnp.where(valid, s, -jnp.inf)
    m = jnp.max(s, axis=-1, keepdims=True)
    p = jnp.exp(s - m)
    denom = jnp.sum(p, axis=-1, keepdims=True)
    o = jnp.einsum('bhrnqk,bhrnkd->bhrnqd', p, vw.astype(jnp.float32)) / denom
    lse = (m + jnp.log(denom))[..., 0]
    o = o.reshape(B, H, dilation, n_sub, dh).transpose(0, 1, 3, 2, 4).reshape(B, H, t_pad, dh)[:, :, :T]
    lse = lse.reshape(B, H, dilation, n_sub).transpose(0, 1, 3, 2).reshape(B, H, t_pad)[:, :, :T]
    return o, lse


def stick_breaking_attention(q, k, v):
    B, H, T, dh = q.shape
    n_blk = T // Q_BLOCK
    qb = q.reshape(B, H, n_blk, Q_BLOCK, dh).transpose(2, 0, 1, 3, 4)
    kpos = jnp.arange(T)
    vf = v.astype(jnp.float32)

    def block(args):
        q_blk, b = args
        z = jnp.einsum('bhqd,bhkd->bhqk', q_blk, k).astype(jnp.float32) * (dh ** -0.5)
        qpos = b * Q_BLOCK + jnp.arange(Q_BLOCK)
        past = kpos[None, :] < qpos[:, None]
        log_beta = jax.nn.log_sigmoid(z)
        log_keep = jnp.where(past, jax.nn.log_sigmoid(-z), 0.0)
        after = lax.cumsum(log_keep, axis=3, reverse=True) - log_keep
        w = jnp.where(past, jnp.exp(log_beta + after), 0.0)
        return jnp.einsum('bhqk,bhkd->bhqd', w, vf)

    o = lax.map(block, (qb, jnp.arange(n_blk)))
    return o.transpose(1, 2, 0, 3, 4).reshape(B, H, T, dh).astype(q.dtype)


def hybrid_mixer(h, w_in, w_proj_dil, w_proj_sb, w_out, cos, sin):
    B, T, _ = h.shape
    proj = h @ w_in
    o1 = 3 * D_DIL
    o2 = o1 + 3 * D_SB
    o3 = o2 + D_MODEL
    qkv_d = proj[..., :o1].reshape(B, T, 3, N_DIL_HEADS, HEAD_DIM).transpose(2, 0, 3, 1, 4)
    q_d = apply_partial_rope(qkv_d[0], cos, sin)
    k_d = apply_partial_rope(qkv_d[1], cos, sin)
    v_d = qkv_d[2]
    outs, lses = [], []
    for g, (window, dilation) in enumerate(DIL_GROUPS):
        hs = slice(g * HEADS_PER_GROUP, (g + 1) * HEADS_PER_GROUP)
        o, lse = dilated_window_attention(q_d[:, hs], k_d[:, hs], v_d[:, hs], window, dilation)
        outs.append(o)
        lses.append(lse)
    w_grp = jax.nn.softmax(jnp.stack(lses, axis=0), axis=0)
    o_dil = jnp.sum(w_grp[..., None] * jnp.stack(outs, axis=0), axis=0).astype(h.dtype)
    o_dil = o_dil.transpose(0, 2, 1, 3).reshape(B, T, D_DIL_OUT)
    qkv_s = proj[..., o1:o2].reshape(B, T, 3, N_SB_HEADS, HEAD_DIM).transpose(2, 0, 3, 1, 4)
    o_sb = stick_breaking_attention(qkv_s[0], qkv_s[1], qkv_s[2])
    o_sb = o_sb.transpose(0, 2, 1, 3).reshape(B, T, D_SB)
    gate_dil = jax.nn.sigmoid(proj[..., o2:o3])
    gate_sb = jax.nn.sigmoid(proj[..., o3:])
    y = gate_dil * (o_dil @ w_proj_dil) + gate_sb * (o_sb @ w_proj_sb)
    return y @ w_out


def setup_inputs(seed: int = 0) -> dict:
    key = jax.random.key(seed)
    ks = jax.random.split(key, 16)
    f32 = jnp.float32

    def dense(k, shape, fan_in):
        return jax.random.normal(k, shape, f32) * (fan_in ** -0.5)

    def gain(k, shape):
        return 1.0 + 0.05 * jax.random.normal(k, shape, f32)

    return {
        "x": jax.random.normal(ks[0], (BATCH, SEQ, D_MODEL), f32),
        "norm_ffn1": gain(ks[1], (DEPTH, D_MODEL)),
        "ffn1_w_gate": dense(ks[2], (DEPTH, D_MODEL, D_FF), D_MODEL),
        "ffn1_w_up": dense(ks[3], (DEPTH, D_MODEL, D_FF), D_MODEL),
        "ffn1_w_down": dense(ks[4], (DEPTH, D_FF, D_MODEL), D_FF),
        "norm_mix": gain(ks[5], (DEPTH, D_MODEL)),
        "w_in": dense(ks[6], (DEPTH, D_MODEL, D_IN), D_MODEL),
        "w_proj_dil": dense(ks[7], (DEPTH, D_DIL_OUT, D_MODEL), D_DIL_OUT),
        "w_proj_sb": dense(ks[8], (DEPTH, D_SB, D_MODEL), D_SB),
        "w_out": dense(ks[9], (DEPTH, D_MODEL, D_MODEL), D_MODEL),
        "norm_ffn2": gain(ks[10], (DEPTH, D_MODEL)),
        "ffn2_w_gate": dense(ks[11], (DEPTH, D_MODEL, D_FF), D_MODEL),
        "ffn2_w_up": dense(ks[12], (DEPTH, D_MODEL, D_FF), D_MODEL),
        "ffn2_w_down": dense(ks[13], (DEPTH, D_FF, D_MODEL), D_FF),
        "norm_final": gain(ks[14], (D_MODEL,)),
    }


def reference(x, norm_ffn1, ffn1_w_gate, ffn1_w_up, ffn1_w_down, norm_mix, w_in,
              w_proj_dil, w_proj_sb, w_out, norm_ffn2, ffn2_w_gate, ffn2_w_up,
              ffn2_w_down, norm_final):
    T = x.shape[1]
    cos, sin = rope_tables(T)
    for l in range(DEPTH):
        x = x + 0.5 * swiglu(rms_norm(x, norm_ffn1[l]), ffn1_w_gate[l], ffn1_w_up[l], ffn1_w_down[l])
        x = x + hybrid_mixer(rms_norm(x, norm_mix[l]), w_in[l], w_proj_dil[l], w_proj_sb[l],
                             w_out[l], cos, sin)
        x = x + 0.5 * swiglu(rms_norm(x, norm_ffn2[l]), ffn2_w_gate[l], ffn2_w_up[l], ffn2_w_down[l])
    return rms_norm(x, norm_final)
```

```python
import functools

import jax
import jax.numpy as jnp
from jax import lax
from jax.experimental import pallas as pl
from jax.experimental.pallas import tpu as pltpu

F32 = jnp.float32
BF16 = jnp.bfloat16

HEAD_DIM = 64
DILATIONS = (1, 4, 16)
SPAN = 128
HEADS_PER_GROUP = 4
N_SB_HEADS = 4
ROPE_DIM = HEAD_DIM // 4
ROPE_THETA = 500000.0
RMS_EPS = 1e-6
LANES = 128
GROUP_W = 3 * HEADS_PER_GROUP * HEAD_DIM
QK_SCALE = HEAD_DIM ** -0.5

VMEM_LIMIT = 56 * 1024 * 1024


def _rms(x, gain):
    ms = jnp.mean(x * x, axis=-1, keepdims=True)
    return x * lax.rsqrt(ms + RMS_EPS) * gain


def _sigmoid(x):
    return 1.0 / (1.0 + jnp.exp(-x))


def _resident(shape):
    return pl.BlockSpec(shape, lambda *_: (0,) * len(shape), pipeline_mode=pl.Buffered(1))


def _ffn_kernel(x_ref, gain_ref, wg_ref, wu_ref, wd_ref, fin_ref, o_ref, *, fc, final):
    x = x_ref[...]
    h = _rms(x, gain_ref[...]).astype(BF16)
    n_ff = wg_ref.shape[1]
    acc = jnp.zeros(x.shape, F32)
    for c in range(n_ff // fc):
        cols = slice(c * fc, (c + 1) * fc)
        g = jnp.dot(h, wg_ref[:, cols], preferred_element_type=F32)
        u = jnp.dot(h, wu_ref[:, cols], preferred_element_type=F32)
        a = (g * _sigmoid(g) * u).astype(BF16)
        acc = acc + jnp.dot(a, wd_ref[cols, :], preferred_element_type=F32)
    y = x + 0.5 * acc
    if final:
        y = _rms(y, fin_ref[...])
    o_ref[...] = y


def _ffn(x2, gain, wg, wu, wd, fin_gain, *, final, tm=512, fc=256):
    n, d = x2.shape
    f = wg.shape[1]
    assert n % tm == 0 and f % fc == 0
    row = pl.BlockSpec((tm, d), lambda i: (i, 0))
    return pl.pallas_call(
        functools.partial(_ffn_kernel, fc=fc, final=final),
        grid=(n // tm,),
        in_specs=[row, _resident((1, d)), _resident((d, f)), _resident((d, f)),
                  _resident((f, d)), _resident((1, d))],
        out_specs=row,
        out_shape=jax.ShapeDtypeStruct((n, d), F32),
        compiler_params=pltpu.CompilerParams(
            dimension_semantics=("parallel",), vmem_limit_bytes=VMEM_LIMIT),
        name="ffn_final" if final else "ffn",
    )(x2, gain, wg, wu, wd, fin_gain)


def _rope(xs, tab_ref):
    return (xs * tab_ref[0]
            + pltpu.roll(xs, LANES - ROPE_DIM // 2, 1) * tab_ref[1]
            + pltpu.roll(xs, ROPE_DIM // 2, 1) * tab_ref[2])


def _emit_group(p, tab_ref, out_ref, d):
    rm = p.shape[0] // d
    n_qk = 2 * HEADS_PER_GROUP * HEAD_DIM // LANES
    for c in range(GROUP_W // LANES):
        cols = slice(c * LANES, (c + 1) * LANES)
        xs = p[:, cols]
        if c < n_qk:
            xs = _rope(xs, tab_ref)
        if c < n_qk // 2:
            xs = xs * QK_SCALE
        xb = xs.astype(BF16)
        for r in range(d):
            out_ref[0, r, :, cols] = xb[r * rm:(r + 1) * rm, :]


def _qkv_kernel(x_ref, gain_ref, wn_ref, w1_ref, w2_ref, tab0_ref, tab1_ref, tab2_ref,
                g0_ref, g1_ref, g2_ref, sb_ref, h_scr, hp_scr):
    tm = x_ref.shape[1]
    h = _rms(x_ref[0], gain_ref[...])
    n_slab = h.shape[1] // LANES
    for c in range(n_slab):
        h_scr[c] = h[:, c * LANES:(c + 1) * LANES]
    pn =jnp.dot(h.astype(BF16), wn_ref[...], preferred_element_type=F32)
    _emit_group(pn[:, :GROUP_W], tab0_ref, g0_ref, 1)
    sb_w = 3 * N_SB_HEADS * HEAD_DIM
    n_q = N_SB_HEADS * HEAD_DIM // LANES
    for c in range(sb_w // LANES):
        xs = pn[:, GROUP_W + c * LANES:GROUP_W + (c + 1) * LANES]
        if c < n_q:
            xs = xs * QK_SCALE
        sb_ref[0, :, c * LANES:(c + 1) * LANES] = xs.astype(BF16)
    for d, w_ref, tab_ref, out_ref in ((DILATIONS[1], w1_ref, tab1_ref, g1_ref),
                                       (DILATIONS[2], w2_ref, tab2_ref, g2_ref)):
        rm = tm // d
        for r in range(d):
            for c in range(n_slab):
                hp_scr[r * rm:(r + 1) * rm, c * LANES:(c + 1) * LANES] = (
                    h_scr[c, pl.ds(r, rm, stride=d), :].astype(BF16))
        p = jnp.dot(hp_scr[...], w_ref[...], preferred_element_type=F32)
        _emit_group(p, tab_ref, out_ref, d)


def _qkv(x, gain, wn, w1, w2, tabs, *, tm=512):
    b, t, dm = x.shape
    nt = t // tm
    d1, d2 = DILATIONS[1], DILATIONS[2]
    assert t % tm == 0 and tm % (16 * d2) == 0
    tab_spec = pl.BlockSpec((3, tm, LANES), lambda i, bb: (0, i, 0))
    sbw = 3 * N_SB_HEADS * HEAD_DIM
    return pl.pallas_call(
        _qkv_kernel,
        grid=(nt, b),
        in_specs=[pl.BlockSpec((1, tm, dm), lambda i, bb: (bb, i, 0)),
                  _resident((1, dm)), _resident(wn.shape), _resident(w1.shape),
                  _resident(w2.shape), tab_spec, tab_spec, tab_spec],
        out_specs=[pl.BlockSpec((1, 1, tm, GROUP_W), lambda i, bb: (bb, 0, i, 0)),
                   pl.BlockSpec((1, d1, tm // d1, GROUP_W), lambda i, bb: (bb, 0, i, 0)),
                   pl.BlockSpec((1, d2, tm // d2, GROUP_W), lambda i, bb: (bb, 0, i, 0)),
                   pl.BlockSpec((1, tm, sbw), lambda i, bb: (bb, i, 0))],
        out_shape=[jax.ShapeDtypeStruct((b, 1, t, GROUP_W), BF16),
                   jax.ShapeDtypeStruct((b, d1, t // d1, GROUP_W), BF16),
                   jax.ShapeDtypeStruct((b, d2, t // d2, GROUP_W), BF16),
                   jax.ShapeDtypeStruct((b, t, sbw), BF16)],
        scratch_shapes=[pltpu.VMEM((dm // LANES, tm, LANES), F32), pltpu.VMEM((tm, dm), BF16)],
        compiler_params=pltpu.CompilerParams(
            dimension_semantics=("parallel", "parallel"), vmem_limit_bytes=VMEM_LIMIT),
        name="qkv_proj",
    )(x, gain, wn, w1, w2, *tabs)


def _nt_dot(a, b):
    return lax.dot_general(a, b, (((1,), (1,)), ((), ())), preferred_element_type=F32)


def _dil_kernel(q0, k0, v0, q1, k1, v1, q2, k2, v2, o_ref, o_scr, l_scr):
    t = o_ref.shape[1]
    lane = lax.broadcasted_iota(jnp.int32, (1, LANES), 1)
    mask_a = (lane < HEAD_DIM).astype(BF16)
    mask_b = (lane >= HEAD_DIM).astype(BF16)
    head_a = lane < HEAD_DIM
    base = (lax.broadcasted_iota(jnp.int32, (SPAN, 2 * SPAN), 0)
            - lax.broadcasted_iota(jnp.int32, (SPAN, 2 * SPAN), 1))

    for g, (q_ref, k_ref, v_ref) in enumerate(((q0, k0, v0), (q1, k1, v1), (q2, k2, v2))):
        d = DILATIONS[g]
        nb = t // d // SPAN

        def block(idx, carry, g=g, d=d, nb=nb, q_ref=q_ref, k_ref=k_ref, v_ref=v_ref):
            r = idx // nb
            n = idx % nb
            ws = jnp.maximum(n - 1, 0) * SPAN
            q = q_ref[0, r, pl.ds(pl.multiple_of(n * SPAN, SPAN), SPAN), :]
            kw = k_ref[0, r, pl.ds(pl.multiple_of(ws, SPAN), 2 * SPAN), :]
            vw = v_ref[0, r, pl.ds(pl.multiple_of(ws, SPAN), 2 * SPAN), :]
            dist = base + (n * SPAN - ws)
            valid = (dist >= 0) & (dist <= SPAN)
            outs, lses = [], []
            for m_h in (mask_a, mask_b):
                s = _nt_dot(q * m_h, kw)
                s = jnp.where(valid, s, -1e30)
                m = jnp.max(s, axis=1, keepdims=True)
                p = jnp.exp(s - m)
                den = jnp.sum(p, axis=1, keepdims=True)
                o = jnp.dot(p.astype(BF16), vw, preferred_element_type=F32) / den
                outs.append(o)
                lses.append(jnp.broadcast_to(m + jnp.log(den), (SPAN, LANES)))
            o = jnp.where(head_a, outs[0], outs[1])
            lse = jnp.where(head_a, lses[0], lses[1])
            start = r + d * n * SPAN
            rows = pl.ds(start, SPAN) if d == 1 else pl.ds(start, SPAN, stride=d)
            o_scr[g, rows, :] = o
            l_scr[g, rows, :] = lse
            return carry

        lax.fori_loop(0, d * nb, block, 0)

    chunk = 256

    def merge(c, carry):
        rows = pl.ds(pl.multiple_of(c * chunk, chunk), chunk)
        l0, l1, l2 = l_scr[0, rows, :], l_scr[1, rows, :], l_scr[2, rows, :]
        m = jnp.maximum(jnp.maximum(l0, l1), l2)
        e0, e1, e2 = jnp.exp(l0 - m), jnp.exp(l1 - m), jnp.exp(l2 - m)
        num = e0 * o_scr[0, rows, :] + e1 * o_scr[1, rows, :] + e2 * o_scr[2, rows, :]
        o_ref[0, rows, :] = (num / (e0 + e1 + e2)).astype(BF16)
        return carry

    lax.fori_loop(0, t // chunk, merge, 0)


def _dil_attention(g0, g1, g2):
    b, _, t, _ = g0.shape
    assert t % (2 * SPAN * DILATIONS[-1]) == 0
    n_pairs = HEADS_PER_GROUP * HEAD_DIM // LANES
    in_specs, args = [], []
    for arr in (g0, g1, g2):
        d, ln = arr.shape[1], arr.shape[2]
        for part in range(3):
            in_specs.append(pl.BlockSpec(
                (1, d, ln, LANES),
                lambda bb, jp, part=part: (bb, 0, 0, part * n_pairs + jp)))
            args.append(arr)
    return pl.pallas_call(
        _dil_kernel,
        grid=(b, n_pairs),
        in_specs=in_specs,
        out_specs=pl.BlockSpec((1, t, LANES), lambda bb, jp: (bb, 0, jp)),
        out_shape=jax.ShapeDtypeStruct((b, t, n_pairs * LANES), BF16),
        scratch_shapes=[pltpu.VMEM((3, t, LANES), F32), pltpu.VMEM((3, t, LANES), F32)],
        compiler_params=pltpu.CompilerParams(
            dimension_semantics=("parallel", "parallel"), vmem_limit_bytes=VMEM_LIMIT),
        name="dilated_attn",
    )(*args)


def _sb_kernel(q_ref, k_ref, v_ref, o_ref, *, tile):
    i = pl.program_id(2)
    lane = lax.broadcasted_iota(jnp.int32, (1, LANES), 1)
    head_a = lane < HEAD_DIM
    q = q_ref[0]
    q_heads = (q * head_a.astype(BF16), q * (lane >= HEAD_DIM).astype(BF16))
    row = lax.broadcasted_iota(jnp.int32, (tile, tile), 0)
    col = lax.broadcasted_iota(jnp.int32, (tile, tile), 1)
    tri = (row >= col).astype(BF16)
    past = col < row

    def sweep(j, carry, diag):
        start = pl.multiple_of(j * tile, tile)
        k = k_ref[0, pl.ds(start, tile), :]
        v = v_ref[0, pl.ds(start, tile), :]
        new = []
        for h in range(2):
            acc, run = carry[2 * h], carry[2 * h + 1]
            z = _nt_dot(q_heads[h], k)
            sp = jnp.maximum(z, 0.0) + jnp.log(1.0 + jnp.exp(-jnp.abs(z)))
            if diag:
                sp = jnp.where(past, sp, 0.0)
            hi = sp.astype(BF16)
            lo = (sp - hi.astype(F32)).astype(BF16)
            csum = (jnp.dot(hi, tri, preferred_element_type=F32)
                    + jnp.dot(lo, tri, preferred_element_type=F32))
            w = jnp.exp(z - csum - run)
            if diag:
                w = jnp.where(past, w, 0.0)
            acc = acc + jnp.dot(w.astype(BF16), v, preferred_element_type=F32)
            run = run + jnp.sum(sp, axis=1, keepdims=True)
            new += [acc, run]
        return tuple(new)

    zero = (jnp.zeros((tile, LANES), F32), jnp.zeros((tile, 1), F32)) * 2
    carry = sweep(i, zero, True)
    carry = lax.fori_loop(0, i, lambda jj, c: sweep(i - 1 - jj, c, False), carry)
    o_ref[0] = jnp.where(head_a, carry[0], carry[2]).astype(BF16)


def _sb_attention(sb, *, tile=256):
    b, t, _ = sb.shape
    n_pairs = N_SB_HEADS * HEAD_DIM // LANES
    assert t % tile == 0
    kv = lambda part: pl.BlockSpec((1, t, LANES),
                                   lambda bb, jp, i: (bb, 0, part * n_pairs + jp))
    return pl.pallas_call(
        functools.partial(_sb_kernel, tile=tile),
        grid=(b, n_pairs, t // tile),
        in_specs=[pl.BlockSpec((1, tile, LANES), lambda bb, jp, i: (bb, i, jp)),
                  kv(1), kv(2)],
        out_specs=pl.BlockSpec((1, tile, LANES), lambda bb, jp, i: (bb, i, jp)),
        out_shape=jax.ShapeDtypeStruct((b, t, n_pairs * LANES), BF16),
        compiler_params=pltpu.CompilerParams(
            dimension_semantics=("parallel", "parallel", "arbitrary"),
            vmem_limit_bytes=VMEM_LIMIT),
        name="stick_breaking_attn",
    )(sb, sb, sb)


def _out_kernel(x_ref, gain_ref, od_ref, os_ref, wgd_ref, wgs_ref, wpd_ref, wps_ref,
                wo_ref, o_ref):
    x = x_ref[...]
    h = _rms(x, gain_ref[...]).astype(BF16)
    y = _sigmoid(jnp.dot(h, wgd_ref[...], preferred_element_type=F32)) * jnp.dot(
        od_ref[...], wpd_ref[...], preferred_element_type=F32)
    y = y + _sigmoid(jnp.dot(h, wgs_ref[...], preferred_element_type=F32)) * jnp.dot(
        os_ref[...], wps_ref[...], preferred_element_type=F32)
    o_ref[...] = x + jnp.dot(y.astype(BF16), wo_ref[...], preferred_element_type=F32)


def _mix_out(x2, gain, o_dil, o_sb, wgd, wgs, wpd, wps, wo, *, tm=512):
    n, d = x2.shape
    row = lambda w: pl.BlockSpec((tm, w), lambda i: (i, 0))
    return pl.pallas_call(
        _out_kernel,
        grid=(n // tm,),
        in_specs=[row(d), _resident((1, d)), row(o_dil.shape[1]), row(o_sb.shape[1]),
                  _resident(wgd.shape), _resident(wgs.shape), _resident(wpd.shape),
                  _resident(wps.shape), _resident(wo.shape)],
        out_specs=row(d),
        out_shape=jax.ShapeDtypeStruct((n, d), F32),
        compiler_params=pltpu.CompilerParams(
            dimension_semantics=("parallel",), vmem_limit_bytes=VMEM_LIMIT),
        name="mix_out",
    )(x2, gain, o_dil, o_sb, wgd, wgs, wpd, wps, wo)


def _rope_tables(t, tm):
    half = ROPE_DIM // 2
    inv_freq = ROPE_THETA ** (-jnp.arange(0, ROPE_DIM, 2, dtype=F32) / ROPE_DIM)
    lane = jnp.arange(LANES) % HEAD_DIM
    tabs = []
    for d in DILATIONS:
        pos = jnp.arange(t, dtype=jnp.int32).reshape(t // tm, tm // d, d)
        pos = pos.transpose(0, 2, 1).reshape(t).astype(F32)
        ang = pos[:, None] * inv_freq[None, :]
        cos = jnp.cos(ang)[:, lane % half]
        sin = jnp.sin(ang)[:, lane % half]
        c = jnp.where(lane < ROPE_DIM, cos, 1.0)
        s1 = jnp.where(lane < half, -sin, 0.0)
        s2 = jnp.where((lane >= half) & (lane < ROPE_DIM), sin, 0.0)
        tabs.append(jnp.stack([c, s1, s2]).astype(F32))
    return tabs


def kernel(x, norm_ffn1, ffn1_w_gate, ffn1_w_up, ffn1_w_down, norm_mix, w_in,
           w_proj_dil, w_proj_sb, w_out, norm_ffn2, ffn2_w_gate, ffn2_w_up,
           ffn2_w_down, norm_final):
    b, t, dm = x.shape
    depth = w_in.shape[0]
    tm_qkv = 512
    tabs = _rope_tables(t, tm_qkv)
    d_dil = len(DILATIONS) * HEADS_PER_GROUP * HEAD_DIM
    gw = HEADS_PER_GROUP * HEAD_DIM
    o1 = 3 * d_dil
    o2 = o1 + 3 * N_SB_HEADS * HEAD_DIM
    o3 = o2 + dm
    bf = lambda a: a.astype(BF16)
    fin = norm_final.reshape(1, dm)

    def group_cols(w, g):
        return jnp.concatenate(
            [w[:, p * d_dil + g * gw: p * d_dil + (g + 1) * gw] for p in range(3)], axis=1)

    x2 = x.reshape(b * t, dm)
    for l in range(depth):
        x2 = _ffn(x2, norm_ffn1[l].reshape(1, dm), bf(ffn1_w_gate[l]), bf(ffn1_w_up[l]),
                  bf(ffn1_w_down[l]), fin, final=False)
        wl = w_in[l]
        wn = bf(jnp.concatenate([group_cols(wl, 0), wl[:, o1:o2]], axis=1))
        g0, g1, g2, sb = _qkv(x2.reshape(b, t, dm), norm_mix[l].reshape(1, dm), wn,
                              bf(group_cols(wl, 1)), bf(group_cols(wl, 2)), tabs, tm=tm_qkv)
        o_dil = _dil_attention(g0, g1, g2)
        o_sb = _sb_attention(sb)
        x2 = _mix_out(x2, norm_mix[l].reshape(1, dm), o_dil.reshape(b * t, -1),
                      o_sb.reshape(b * t, -1), bf(wl[:, o2:o3]), bf(wl[:, o3:]),
                      bf(w_proj_dil[l]), bf(w_proj_sb[l]), bf(w_out[l]))
        x2 = _ffn(x2, norm_ffn2[l].reshape(1, dm), bf(ffn2_w_gate[l]), bf(ffn2_w_up[l]),
                  bf(ffn2_w_down[l]), fin, final=(l == depth - 1))
    return x2.reshape(b, t, dm)
```

```python
import functools

import jax
import jax.numpy as jnp
from jax import lax
from jax.experimental import pallas as pl
from jax.experimental.pallas import tpu as pltpu

F32 = jnp.float32
BF16 = jnp.bfloat16

HEAD_DIM = 64
DILATIONS = (1, 4, 16)
SPAN = 128
HEADS_PER_GROUP = 4
N_SB_HEADS = 4
ROPE_DIM = HEAD_DIM // 4
ROPE_THETA = 500000.0
RMS_EPS = 1e-6
LANES = 128
GROUP_W = 3 * HEADS_PER_GROUP * HEAD_DIM
QK_SCALE = HEAD_DIM ** -0.5

SB_EXIT = 105.0

VMEM_LIMIT =56 * 1024 * 1024


def _rms(x, gain):
    ms = jnp.mean(x * x, axis=-1, keepdims=True)
    return x * lax.rsqrt(ms + RMS_EPS) * gain


def _sigmoid(x):
    return 1.0 / (1.0 + jnp.exp(-x))


def _resident(shape):
    return pl.BlockSpec(shape, lambda *_: (0,) * len(shape), pipeline_mode=pl.Buffered(1))


def _ffn_kernel(x_ref, gain_ref, wg_ref, wu_ref, wd_ref, fin_ref, o_ref, *, fc, final):
    x = x_ref[...]
    h = _rms(x, gain_ref[...]).astype(BF16)
    n_ff = wg_ref.shape[1]
    acc = jnp.zeros(x.shape, F32)
    for c in range(n_ff // fc):
        cols = slice(c * fc, (c + 1) * fc)
        g = jnp.dot(h, wg_ref[:, cols], preferred_element_type=F32)
        u = jnp.dot(h, wu_ref[:, cols], preferred_element_type=F32)
        a = (g * _sigmoid(g) * u).astype(BF16)
        acc = acc + jnp.dot(a, wd_ref[cols, :], preferred_element_type=F32)
    y = x + 0.5 * acc
    if final:
        y = _rms(y, fin_ref[...])
    o_ref[...] = y


def _ffn(x2, gain, wg, wu, wd, fin_gain, *, final, tm=512, fc=256):
    n, d = x2.shape
    f = wg.shape[1]
    assert n % tm == 0 and f % fc == 0
    row = pl.BlockSpec((tm, d), lambda i: (i, 0))
    return pl.pallas_call(
        functools.partial(_ffn_kernel, fc=fc, final=final),
        grid=(n // tm,),
        in_specs=[row, _resident((1, d)), _resident((d, f)), _resident((d, f)),
                  _resident((f, d)), _resident((1, d))],
        out_specs=row,
        out_shape=jax.ShapeDtypeStruct((n, d), F32),
        compiler_params=pltpu.CompilerParams(
            dimension_semantics=("parallel",), vmem_limit_bytes=VMEM_LIMIT),
        name="ffn_final" if final else "ffn",
    )(x2, gain, wg, wu, wd, fin_gain)


def _rope(xs, tab_ref):
    return (xs * tab_ref[0]
            + pltpu.roll(xs, LANES - ROPE_DIM // 2, 1) * tab_ref[1]
            + pltpu.roll(xs, ROPE_DIM // 2, 1) * tab_ref[2])


def _emit_group(p, tab_ref, out_ref, d):
    rm = p.shape[0] // d
    n_qk = 2 * HEADS_PER_GROUP * HEAD_DIM // LANES
    for c in range(GROUP_W // LANES):
        cols = slice(c * LANES, (c + 1) * LANES)
        xs = p[:, cols]
        if c < n_qk:
            xs = _rope(xs, tab_ref)
        if c < n_qk // 2:
            xs = xs * QK_SCALE
        xb = xs.astype(BF16)
        for r in range(d):
            out_ref[0, r, :, cols] = xb[r * rm:(r + 1) * rm, :]


def _qkv_kernel(x_ref, gain_ref, wn_ref, w1_ref, w2_ref, tab0_ref, tab1_ref, tab2_ref,
                g0_ref, g1_ref, g2_ref, sb_ref, h_scr, hp_scr):
    tm = x_ref.shape[1]
    h = _rms(x_ref[0], gain_ref[...])
    n_slab = h.shape[1] // LANES
    for c in range(n_slab):
        h_scr[c] = h[:, c * LANES:(c + 1) * LANES]
    pn =jnp.dot(h.astype(BF16), wn_ref[...], preferred_element_type=F32)
    _emit_group(pn[:, :GROUP_W], tab0_ref, g0_ref, 1)
    sb_w = 3 * N_SB_HEADS * HEAD_DIM
    n_q = N_SB_HEADS * HEAD_DIM // LANES
    for c in range(sb_w // LANES):
        xs = pn[:, GROUP_W + c * LANES:GROUP_W + (c + 1) * LANES]
        if c < n_q:
            xs = xs * QK_SCALE
        sb_ref[0, :, c * LANES:(c + 1) * LANES] = xs.astype(BF16)
    for d, w_ref, tab_ref, out_ref in ((DILATIONS[1], w1_ref, tab1_ref, g1_ref),
                                       (DILATIONS[2], w2_ref, tab2_ref, g2_ref)):
        rm = tm // d
        for r in range(d):
            for c in range(n_slab):
                hp_scr[r * rm:(r + 1) * rm, c * LANES:(c + 1) * LANES] = (
                    h_scr[c, pl.ds(r, rm, stride=d), :].astype(BF16))
        p = jnp.dot(hp_scr[...], w_ref[...], preferred_element_type=F32)
        _emit_group(p, tab_ref, out_ref, d)


def _qkv(x, gain, wn, w1, w2, tabs, *, tm=512):
    b, t, dm = x.shape
    nt = t // tm
    d1, d2 = DILATIONS[1], DILATIONS[2]
    assert t % tm == 0 and tm % (16 * d2) == 0
    tab_spec = pl.BlockSpec((3, tm, LANES), lambda i, bb: (0, i, 0))
    sbw = 3 * N_SB_HEADS * HEAD_DIM
    return pl.pallas_call(
        _qkv_kernel,
        grid=(nt, b),
        in_specs=[pl.BlockSpec((1, tm, dm), lambda i, bb: (bb, i, 0)),
                  _resident((1, dm)), _resident(wn.shape), _resident(w1.shape),
                  _resident(w2.shape), tab_spec, tab_spec, tab_spec],
        out_specs=[pl.BlockSpec((1, 1, tm, GROUP_W), lambda i, bb: (bb, 0, i, 0)),
                   pl.BlockSpec((1, d1, tm // d1, GROUP_W), lambda i, bb: (bb, 0, i, 0)),
                   pl.BlockSpec((1, d2, tm // d2, GROUP_W), lambda i, bb: (bb, 0, i, 0)),
                   pl.BlockSpec((1, tm, sbw), lambda i, bb: (bb, i, 0))],
        out_shape=[jax.ShapeDtypeStruct((b, 1, t, GROUP_W), BF16),
                   jax.ShapeDtypeStruct((b, d1, t // d1, GROUP_W), BF16),
                   jax.ShapeDtypeStruct((b, d2, t // d2, GROUP_W), BF16),
                   jax.ShapeDtypeStruct((b, t, sbw), BF16)],
        scratch_shapes=[pltpu.VMEM((dm // LANES, tm, LANES), F32), pltpu.VMEM((tm, dm), BF16)],
        compiler_params=pltpu.CompilerParams(
            dimension_semantics=("parallel", "parallel"), vmem_limit_bytes=VMEM_LIMIT),
        name="qkv_proj",
    )(x, gain, wn, w1, w2, *tabs)


def _nt_dot(a, b):
    return lax.dot_general(a, b, (((1,), (1,)), ((), ())), preferred_element_type=F32)


def _dil_kernel(q0, k0, v0, q1, k1, v1, q2, k2, v2, o_ref, o_scr, l_scr):
    t = o_ref.shape[1]
    lane = lax.broadcasted_iota(jnp.int32, (1, LANES), 1)
    mask_a = (lane < HEAD_DIM).astype(BF16)
    mask_b = (lane >= HEAD_DIM).astype(BF16)
    head_a = lane < HEAD_DIM
    base = (lax.broadcasted_iota(jnp.int32, (SPAN, 2 * SPAN), 0)
            - lax.broadcasted_iota(jnp.int32, (SPAN, 2 * SPAN), 1))

    for g, (q_ref, k_ref, v_ref) in enumerate(((q0, k0, v0), (q1, k1, v1), (q2, k2, v2))):
        d = DILATIONS[g]
        nb = t // d // SPAN

        def block(idx, carry, g=g, d=d, nb=nb, q_ref=q_ref, k_ref=k_ref, v_ref=v_ref):
            r = idx // nb
            n = idx % nb
            ws = jnp.maximum(n - 1, 0) * SPAN
            q = q_ref[0, r, pl.ds(pl.multiple_of(n * SPAN, SPAN), SPAN), :]
            kw = k_ref[0, r, pl.ds(pl.multiple_of(ws, SPAN), 2 * SPAN), :]
            vw = v_ref[0, r, pl.ds(pl.multiple_of(ws, SPAN), 2 * SPAN), :]
            dist = base + (n * SPAN - ws)
            valid = (dist >= 0) & (dist <= SPAN)
            outs, lses = [], []
            for m_h in (mask_a, mask_b):
                s = _nt_dot(q * m_h, kw)
                s = jnp.where(valid, s, -1e30)
                m = jnp.max(s, axis=1, keepdims=True)
                p = jnp.exp(s - m)
                den = jnp.sum(p, axis=1, keepdims=True)
                o = jnp.dot(p.astype(BF16), vw, preferred_element_type=F32) / den
                outs.append(o)
                lses.append(jnp.broadcast_to(m + jnp.log(den), (SPAN, LANES)))
            o = jnp.where(head_a, outs[0], outs[1])
            lse = jnp.where(head_a, lses[0], lses[1])
            start = r + d * n * SPAN
            rows = pl.ds(start, SPAN) if d == 1 else pl.ds(start, SPAN, stride=d)
            o_scr[g, rows, :] = o
            l_scr[g, rows, :] = lse
            return carry

        lax.fori_loop(0, d * nb, block, 0)

    chunk = 256

    def merge(c, carry):
        rows = pl.ds(pl.multiple_of(c * chunk, chunk), chunk)
        l0, l1, l2 = l_scr[0, rows, :], l_scr[1, rows, :], l_scr[2, rows, :]
        m = jnp.maximum(jnp.maximum(l0, l1), l2)
        e0, e1, e2 = jnp.exp(l0 - m), jnp.exp(l1 - m), jnp.exp(l2 - m)
        num = e0 * o_scr[0, rows, :] + e1 * o_scr[1, rows, :] + e2 * o_scr[2, rows, :]
        o_ref[0, rows, :] = (num / (e0 + e1 + e2)).astype(BF16)
        return carry

    lax.fori_loop(0, t // chunk, merge, 0)


def _dil_attention(g0, g1, g2):
    b, _, t, _ = g0.shape
    assert t % (2 * SPAN * DILATIONS[-1]) == 0
    n_pairs = HEADS_PER_GROUP * HEAD_DIM // LANES
    in_specs, args = [], []
    for arr in (g0, g1, g2):
        d, ln = arr.shape[1], arr.shape[2]
        for part in range(3):
            in_specs.append(pl.BlockSpec(
                (1, d, ln, LANES),
                lambda bb, jp, part=part: (bb, 0, 0, part * n_pairs + jp)))
            args.append(arr)
    return pl.pallas_call(
        _dil_kernel,
        grid=(b, n_pairs),
        in_specs=in_specs,
        out_specs=pl.BlockSpec((1, t, LANES), lambda bb, jp: (bb, 0, jp)),
        out_shape=jax.ShapeDtypeStruct((b, t, n_pairs * LANES), BF16),
        scratch_shapes=[pltpu.VMEM((3, t, LANES), F32), pltpu.VMEM((3, t, LANES), F32)],
        compiler_params=pltpu.CompilerParams(
            dimension_semantics=("parallel", "parallel"), vmem_limit_bytes=VMEM_LIMIT),
        name="dilated_attn",
    )(*args)


def _sb_kernel(q_ref, k_ref, v_ref, o_ref, *, tile):
    i = pl.program_id(2)
    lane = lax.broadcasted_iota(jnp.int32, (1, LANES), 1)
    head_a = lane < HEAD_DIM
    q = q_ref[0]
    q_heads = (q * head_a.astype(BF16), q * (lane >= HEAD_DIM).astype(BF16))
    row = lax.broadcasted_iota(jnp.int32, (tile, tile), 0)
    col = lax.broadcasted_iota(jnp.int32, (tile, tile), 1)
    tri = (row >= col).astype(BF16)
    tri2 = jnp.concatenate([tri, tri], axis=0)
    past = col < row

    def sweep(j, carry, diag):
        start = pl.multiple_of(j * tile, tile)
        k = k_ref[0, pl.ds(start, tile), :]
        v = v_ref[0, pl.ds(start, tile), :]
        new = []
        for h in range(2):
            acc, run = carry[2 * h], carry[2 * h + 1]
            z = _nt_dot(q_heads[h], k)
            sp = jnp.maximum(z, 0.0) + jnp.log(1.0 + jnp.exp(-jnp.abs(z)))
            if diag:
                sp = jnp.where(past, sp, 0.0)
            hi = sp.astype(BF16)
            lo = (sp - hi.astype(F32)).astype(BF16)
            csum = jnp.dot(jnp.concatenate([hi, lo], axis=1), tri2,
                           preferred_element_type=F32)
            w = jnp.exp(z - csum - run)
            if diag:
                w = jnp.where(past, w, 0.0)
            acc = acc + jnp.dot(w.astype(BF16), v, preferred_element_type=F32)
            run = run + jnp.sum(sp, axis=1, keepdims=True)
            new += [acc, run]
        return tuple(new)

    def live(carry):
        low = jnp.minimum(jnp.min(carry[1]), jnp.min(carry[3]))
        return (low < SB_EXIT).astype(jnp.int32)

    def step(c):
        new = sweep(i - 1 - c[0], c[2:], False)
        return (c[0] + 1, live(new)) + new

    zero = (jnp.zeros((tile, LANES), F32), jnp.zeros((tile, 1), F32)) * 2
    first = sweep(i, zero, True)
    out = lax.while_loop(lambda c: (c[0] < i) & (c[1] > 0), step,
                         (jnp.int32(0), live(first)) + first)
    o_ref[0] = jnp.where(head_a, out[2], out[4]).astype(BF16)


def _sb_attention(sb, *, tile=256):
    b, t, _ = sb.shape
    n_pairs = N_SB_HEADS * HEAD_DIM // LANES
    assert t % tile == 0
    kv = lambda part: pl.BlockSpec((1, t, LANES),
                                   lambda bb, jp, i: (bb, 0, part * n_pairs + jp))
    return pl.pallas_call(
        functools.partial(_sb_kernel, tile=tile),
        grid=(b, n_pairs, t // tile),
        in_specs=[pl.BlockSpec((1, tile, LANES), lambda bb, jp, i: (bb, i, jp)),
                  kv(1), kv(2)],
        out_specs=pl.BlockSpec((1, tile, LANES), lambda bb, jp, i: (bb, i, jp)),
        out_shape=jax.ShapeDtypeStruct((b, t, n_pairs * LANES), BF16),
        compiler_params=pltpu.CompilerParams(
            dimension_semantics=("parallel", "parallel", "arbitrary"),
            vmem_limit_bytes=VMEM_LIMIT),
        name="stick_breaking_attn",
    )(sb, sb, sb)


def _out_kernel(x_ref, gain_ref, od_ref, os_ref, wgd_ref, wgs_ref, wpd_ref, wps_ref,
                wo_ref, o_ref):
    x = x_ref[...]
    h = _rms(x, gain_ref[...]).astype(BF16)
    y = _sigmoid(jnp.dot(h, wgd_ref[...], preferred_element_type=F32)) * jnp.dot(
        od_ref[...], wpd_ref[...], preferred_element_type=F32)
    y = y + _sigmoid(jnp.dot(h, wgs_ref[...], preferred_element_type=F32)) * jnp.dot(
        os_ref[...], wps_ref[...], preferred_element_type=F32)
    o_ref[...] = x + jnp.dot(y.astype(BF16), wo_ref[...], preferred_element_type=F32)


def _mix_out(x2, gain, o_dil, o_sb, wgd, wgs, wpd, wps, wo, *, tm=512):
    n, d = x2.shape
    row = lambda w: pl.BlockSpec((tm, w), lambda i: (i, 0))
    return pl.pallas_call(
        _out_kernel,
        grid=(n // tm,),
        in_specs=[row(d), _resident((1, d)), row(o_dil.shape[1]), row(o_sb.shape[1]),
                  _resident(wgd.shape), _resident(wgs.shape), _resident(wpd.shape),
                  _resident(wps.shape), _resident(wo.shape)],
        out_specs=row(d),
        out_shape=jax.ShapeDtypeStruct((n, d), F32),
        compiler_params=pltpu.CompilerParams(
            dimension_semantics=("parallel",), vmem_limit_bytes=VMEM_LIMIT),
        name="mix_out",
    )(x2, gain, o_dil, o_sb, wgd, wgs, wpd, wps, wo)


def _rope_tables(t, tm):
    half = ROPE_DIM // 2
    inv_freq = ROPE_THETA ** (-jnp.arange(0, ROPE_DIM, 2, dtype=F32) / ROPE_DIM)
    lane = jnp.arange(LANES) % HEAD_DIM
    tabs = []
    for d in DILATIONS:
        pos = jnp.arange(t, dtype=jnp.int32).reshape(t // tm, tm // d, d)
        pos = pos.transpose(0, 2, 1).reshape(t).astype(F32)
        ang = pos[:, None] * inv_freq[None, :]
        cos = jnp.cos(ang)[:, lane % half]
        sin = jnp.sin(ang)[:, lane % half]
        c = jnp.where(lane < ROPE_DIM, cos, 1.0)
        s1 = jnp.where(lane < half, -sin, 0.0)
        s2 = jnp.where((lane >= half) & (lane < ROPE_DIM), sin, 0.0)
        tabs.append(jnp.stack([c, s1, s2]).astype(F32))
    return tabs


def kernel(x, norm_ffn1, ffn1_w_gate, ffn1_w_up, ffn1_w_down, norm_mix, w_in,
           w_proj_dil, w_proj_sb, w_out, norm_ffn2, ffn2_w_gate, ffn2_w_up,
           ffn2_w_down, norm_final):
    b, t, dm = x.shape
    depth = w_in.shape[0]
    tm_qkv = 512
    tabs = _rope_tables(t, tm_qkv)
    d_dil = len(DILATIONS) * HEADS_PER_GROUP * HEAD_DIM
    gw = HEADS_PER_GROUP * HEAD_DIM
    o1 = 3 * d_dil
    o2 = o1 + 3 * N_SB_HEADS * HEAD_DIM
    o3 = o2 + dm
    bf = lambda a: a.astype(BF16)
    fin = norm_final.reshape(1, dm)

    def group_cols(w, g):
        return jnp.concatenate(
            [w[:, p * d_dil + g * gw: p * d_dil + (g + 1) * gw] for p in range(3)], axis=1)

    x2 = x.reshape(b * t, dm)
    for l in range(depth):
        x2 = _ffn(x2, norm_ffn1[l].reshape(1, dm), bf(ffn1_w_gate[l]), bf(ffn1_w_up[l]),
                  bf(ffn1_w_down[l]), fin, final=False)
        wl = w_in[l]
        wn = bf(jnp.concatenate([group_cols(wl, 0), wl[:, o1:o2]], axis=1))
        g0, g1, g2, sb = _qkv(x2.reshape(b, t, dm), norm_mix[l].reshape(1, dm), wn,
                              bf(group_cols(wl, 1)), bf(group_cols(wl, 2)), tabs, tm=tm_qkv)
        o_dil = _dil_attention(g0, g1, g2)
        o_sb = _sb_attention(sb)
        x2 = _mix_out(x2, norm_mix[l].reshape(1, dm), o_dil.reshape(b * t, -1),
                      o_sb.reshape(b * t, -1), bf(wl[:, o2:o3]), bf(wl[:, o3:]),
                      bf(w_proj_dil[l]), bf(w_proj_sb[l]), bf(w_out[l]))
        x2 = _ffn(x2, norm_ffn2[l].reshape(1, dm), bf(ffn2_w_gate[l]), bf(ffn2_w_up[l]),
                  bf(ffn2_w_down[l]), fin, final=(l == depth - 1))
    return x2.reshape(b, t, dm)
```

```python
import functools

import jax
import jax.numpy as jnp
from jax import lax
from jax.experimental import pallas as pl
from jax.experimental.pallas import tpu as pltpu

F32 = jnp.float32
BF16 = jnp.bfloat16

HEAD_DIM = 64
DILATIONS = (1, 4, 16)
SPAN = 128
HEADS_PER_GROUP = 4
N_SB_HEADS = 4
ROPE_DIM = HEAD_DIM // 4
ROPE_THETA = 500000.0
RMS_EPS = 1e-6
LANES = 128
GROUP_W = 3 * HEADS_PER_GROUP * HEAD_DIM
QK_SCALE = HEAD_DIM ** -0.5

DIL_UNROLL = 8
SB_EXIT = 105.0

VMEM_LIMIT = 56 * 1024 * 1024


def _rms(x, gain):
    ms = jnp.mean(x * x, axis=-1, keepdims=True)
    return x * lax.rsqrt(ms + RMS_EPS) * gain


def _sigmoid(x):
    return 1.0 / (1.0 + jnp.exp(-x))


def _resident(shape):
    return pl.BlockSpec(shape, lambda *_: (0,) * len(shape), pipeline_mode=pl.Buffered(1))


def _ffn_kernel(x_ref, gain_ref, wg_ref, wu_ref, wd_ref, fin_ref, o_ref, *, fc, final):
    x = x_ref[...]
    h = _rms(x, gain_ref[...]).astype(BF16)
    n_ff = wg_ref.shape[1]
    acc = jnp.zeros(x.shape, F32)
    for c in range(n_ff // fc):
        cols = slice(c * fc, (c + 1) * fc)
        g = jnp.dot(h, wg_ref[:, cols], preferred_element_type=F32)
        u = jnp.dot(h, wu_ref[:, cols], preferred_element_type=F32)
        a = (g * _sigmoid(g) * u).astype(BF16)
        acc = acc + jnp.dot(a, wd_ref[cols, :], preferred_element_type=F32)
    y = x + 0.5 * acc
    if final:
        y = _rms(y, fin_ref[...])
    o_ref[...] = y


def _ffn(x2, gain, wg, wu, wd, fin_gain, *, final, tm=512, fc=256):
    n, d = x2.shape
    f = wg.shape[1]
    assert n % tm == 0 and f % fc == 0
    row = pl.BlockSpec((tm, d), lambda i: (i, 0))
    return pl.pallas_call(
        functools.partial(_ffn_kernel, fc=fc, final=final),
        grid=(n // tm,),
        in_specs=[row, _resident((1, d)), _resident((d, f)), _resident((d, f)),
                  _resident((f, d)), _resident((1, d))],
        out_specs=row,
        out_shape=jax.ShapeDtypeStruct((n, d), F32),
        compiler_params=pltpu.CompilerParams(
            dimension_semantics=("parallel",), vmem_limit_bytes=VMEM_LIMIT),
        name="ffn_final" if final else "ffn",
    )(x2, gain, wg, wu, wd, fin_gain)


def _rope(xs, tab_ref):
    return (xs * tab_ref[0]
            + pltpu.roll(xs, LANES - ROPE_DIM // 2, 1) * tab_ref[1]
            + pltpu.roll(xs, ROPE_DIM // 2, 1) * tab_ref[2])


def _emit_group(p, tab_ref, out_ref, d):
    rm = p.shape[0] // d
    n_qk = 2 * HEADS_PER_GROUP * HEAD_DIM // LANES
    for c in range(GROUP_W // LANES):
        cols = slice(c * LANES, (c + 1) * LANES)
        xs = p[:, cols]
        if c < n_qk:
            xs = _rope(xs, tab_ref)
        if c < n_qk // 2:
            xs = xs * QK_SCALE
        xb = xs.astype(BF16)
        for r in range(d):
            out_ref[0, r, :, cols] = xb[r * rm:(r + 1) * rm, :]


def _qkv_kernel(x_ref, gain_ref, wn_ref, w1_ref, w2_ref, tab0_ref, tab1_ref, tab2_ref,
                g0_ref, g1_ref, g2_ref, sb_ref, h_scr, hp_scr):
    tm = x_ref.shape[1]
    h = _rms(x_ref[0], gain_ref[...])
    n_slab = h.shape[1] // LANES
    for c in range(n_slab):
        h_scr[c] = h[:, c * LANES:(c + 1) * LANES]
    pn = jnp.dot(h.astype(BF16), wn_ref[...], preferred_element_type=F32)
    _emit_group(pn[:, :GROUP_W], tab0_ref, g0_ref, 1)
    sb_w = 3 * N_SB_HEADS * HEAD_DIM
    n_q = N_SB_HEADS * HEAD_DIM // LANES
    for c in range(sb_w // LANES):
        xs = pn[:, GROUP_W + c * LANES:GROUP_W + (c + 1) * LANES]
        if c < n_q:
            xs = xs * QK_SCALE
        sb_ref[0, :, c * LANES:(c + 1) * LANES] = xs.astype(BF16)
    for d, w_ref, tab_ref, out_ref in ((DILATIONS[1], w1_ref, tab1_ref, g1_ref),
                                       (DILATIONS[2], w2_ref, tab2_ref, g2_ref)):
        rm = tm // d
        for r in range(d):
            for c in range(n_slab):
                hp_scr[r * rm:(r + 1) * rm, c * LANES:(c + 1) * LANES] = (
                    h_scr[c, pl.ds(r, rm, stride=d), :].astype(BF16))
        p = jnp.dot(hp_scr[...], w_ref[...], preferred_element_type=F32)
        _emit_group(p, tab_ref, out_ref, d)


def _qkv(x, gain, wn, w1, w2, tabs, *, tm=512):
    b, t, dm = x.shape
    nt = t // tm
    d1, d2 = DILATIONS[1], DILATIONS[2]
    assert t % tm == 0 and tm % (16 * d2) == 0
    tab_spec = pl.BlockSpec((3, tm, LANES), lambda i, bb: (0, i, 0))
    sbw = 3 * N_SB_HEADS * HEAD_DIM
    return pl.pallas_call(
        _qkv_kernel,
        grid=(nt, b),
        in_specs=[pl.BlockSpec((1, tm, dm), lambda i, bb: (bb, i, 0)),
                  _resident((1, dm)), _resident(wn.shape), _resident(w1.shape),
                  _resident(w2.shape), tab_spec, tab_spec, tab_spec],
        out_specs=[pl.BlockSpec((1, 1, tm, GROUP_W), lambda i, bb: (bb, 0, i, 0)),
                   pl.BlockSpec((1, d1, tm // d1, GROUP_W), lambda i, bb: (bb, 0, i, 0)),
                   pl.BlockSpec((1, d2, tm // d2, GROUP_W), lambda i, bb: (bb, 0, i, 0)),
                   pl.BlockSpec((1, tm, sbw), lambda i, bb: (bb, i, 0))],
        out_shape=[jax.ShapeDtypeStruct((b, 1, t, GROUP_W), BF16),
                   jax.ShapeDtypeStruct((b, d1, t // d1, GROUP_W), BF16),
                   jax.ShapeDtypeStruct((b, d2, t // d2, GROUP_W), BF16),
                   jax.ShapeDtypeStruct((b, t, sbw), BF16)],
        scratch_shapes=[pltpu.VMEM((dm // LANES, tm, LANES), F32), pltpu.VMEM((tm, dm), BF16)],
        compiler_params=pltpu.CompilerParams(
            dimension_semantics=("parallel", "parallel"), vmem_limit_bytes=VMEM_LIMIT),
        name="qkv_proj",
    )(x, gain, wn, w1, w2, *tabs)


def _nt_dot(a, b):
    return lax.dot_general(a, b, (((1,), (1,)), ((), ())), preferred_element_type=F32)


def _dil_kernel(q0, k0, v0, q1, k1, v1, q2, k2, v2, o_ref, o_scr, l_scr):
    t = o_ref.shape[1]
    lane = lax.broadcasted_iota(jnp.int32, (1, LANES), 1)
    mask_a = (lane < HEAD_DIM).astype(BF16)
    mask_b = (lane >= HEAD_DIM).astype(BF16)
    head_a = lane < HEAD_DIM
    dist = (lax.broadcasted_iota(jnp.int32, (SPAN, 2 * SPAN), 0)
            - lax.broadcasted_iota(jnp.int32, (SPAN, 2 * SPAN), 1))
    bias_first = jnp.where(dist >= 0, 0.0, -1e30)
    bias_band = jnp.where((dist + SPAN >= 0) & (dist <= 0), 0.0, -1e30)

    for g, (q_ref, k_ref, v_ref) in enumerate(((q0, k0, v0), (q1, k1, v1), (q2, k2, v2))):
        d = DILATIONS[g]
        nb = t // d // SPAN

        def block(idx, carry, g=g, d=d, nb=nb, q_ref=q_ref, k_ref=k_ref, v_ref=v_ref):
            r = idx // nb
            n = idx % nb
            ws = jnp.maximum(n - 1, 0) * SPAN
            q = q_ref[0, r, pl.ds(pl.multiple_of(n * SPAN, SPAN), SPAN), :]
            kw = k_ref[0, r, pl.ds(pl.multiple_of(ws, SPAN), 2 * SPAN), :]
            vw = v_ref[0, r, pl.ds(pl.multiple_of(ws, SPAN), 2 * SPAN), :]
            bias = jnp.where(n == 0, bias_first, bias_band)
            outs, lses = [], []
            for m_h in (mask_a, mask_b):
                s = _nt_dot(q * m_h, kw) + bias
                m = jnp.max(s, axis=1, keepdims=True)
                p = jnp.exp(s - m)
                den = jnp.sum(p, axis=1, keepdims=True)
                o = jnp.dot(p.astype(BF16), vw, preferred_element_type=F32) / den
                outs.append(o)
                lses.append(jnp.broadcast_to(m + jnp.log(den), (SPAN, LANES)))
            o = jnp.where(head_a, outs[0], outs[1])
            lse = jnp.where(head_a, lses[0], lses[1])
            start = r + d * n * SPAN
            rows = pl.ds(start, SPAN) if d == 1 else pl.ds(start, SPAN, stride=d)
            o_scr[g, rows, :] = o
            l_scr[g, rows, :] = lse
            return carry

        lax.fori_loop(0, d * nb, block, 0, unroll=DIL_UNROLL)

    chunk = 256

    def merge(c, carry):
        rows = pl.ds(pl.multiple_of(c * chunk, chunk), chunk)
        l0, l1, l2 = l_scr[0, rows, :], l_scr[1, rows, :], l_scr[2, rows, :]
        m = jnp.maximum(jnp.maximum(l0, l1), l2)
        e0, e1, e2 = jnp.exp(l0 - m), jnp.exp(l1 - m), jnp.exp(l2 - m)
        num = e0 * o_scr[0, rows, :] + e1 * o_scr[1, rows, :] + e2 * o_scr[2, rows, :]
        o_ref[0, rows, :] = (num / (e0 + e1 + e2)).astype(BF16)
        return carry

    lax.fori_loop(0, t // chunk, merge, 0)


def _dil_attention(g0, g1, g2):
    b, _, t, _ = g0.shape
    assert t % (2 * SPAN * DILATIONS[-1]) == 0
    n_pairs = HEADS_PER_GROUP * HEAD_DIM // LANES
    in_specs, args = [], []
    for arr in (g0, g1, g2):
        d, ln = arr.shape[1], arr.shape[2]
        for part in range(3):
            in_specs.append(pl.BlockSpec(
                (1, d, ln, LANES),
                lambda bb, jp, part=part: (bb, 0, 0, part * n_pairs + jp)))
            args.append(arr)
    return pl.pallas_call(
        _dil_kernel,
        grid=(b, n_pairs),
        in_specs=in_specs,
        out_specs=pl.BlockSpec((1, t, LANES), lambda bb, jp: (bb, 0, jp)),
        out_shape=jax.ShapeDtypeStruct((b, t, n_pairs * LANES), BF16),
        scratch_shapes=[pltpu.VMEM((3, t, LANES), F32), pltpu.VMEM((3, t, LANES), F32)],
        compiler_params=pltpu.CompilerParams(
            dimension_semantics=("parallel", "parallel"), vmem_limit_bytes=VMEM_LIMIT),
        name="dilated_attn",
    )(*args)


def _sb_kernel(q_ref, k_ref, v_ref, o_ref, *, tile):
    i = pl.program_id(2)
    lane = lax.broadcasted_iota(jnp.int32, (1, LANES), 1)
    head_a = lane < HEAD_DIM
    q = q_ref[0]
    q_heads = (q * head_a.astype(BF16), q * (lane >= HEAD_DIM).astype(BF16))
    row = lax.broadcasted_iota(jnp.int32, (tile, tile), 0)
    col = lax.broadcasted_iota(jnp.int32, (tile, tile), 1)
    tri = (row >= col).astype(BF16)
    tri2 = jnp.concatenate([tri, tri], axis=0)
    past = col < row

    def sweep(j, carry, diag):
        start = pl.multiple_of(j * tile, tile)
        k = k_ref[0, pl.ds(start, tile), :]
        v = v_ref[0, pl.ds(start, tile), :]
        new = []
        for h in range(2):
            acc, run = carry[2 * h], carry[2 * h + 1]
            z = _nt_dot(q_heads[h], k)
            sp = jnp.maximum(z, 0.0) + jnp.log(1.0 + jnp.exp(-jnp.abs(z)))
            if diag:
                sp = jnp.where(past, sp, 0.0)
            hi = sp.astype(BF16)
            lo = (sp - hi.astype(F32)).astype(BF16)
            csum = jnp.dot(jnp.concatenate([hi, lo], axis=1), tri2,
                           preferred_element_type=F32)
            w = jnp.exp(z - csum - run)
            if diag:
                w = jnp.where(past, w, 0.0)
            acc = acc + jnp.dot(w.astype(BF16), v, preferred_element_type=F32)
            run = run + jnp.sum(sp, axis=1, keepdims=True)
            new += [acc, run]
        return tuple(new)

    def live(carry):
        low = jnp.minimum(jnp.min(carry[1]), jnp.min(carry[3]))
        return (low < SB_EXIT).astype(jnp.int32)

    def step(c):
        new = sweep(i - 1 - c[0], c[2:], False)
        return (c[0] + 1, live(new)) + new

    zero = (jnp.zeros((tile, LANES), F32), jnp.zeros((tile, 1), F32)) * 2
    first = sweep(i, zero, True)
    out = lax.while_loop(lambda c: (c[0] < i) & (c[1] > 0), step,
                         (jnp.int32(0), live(first)) + first)
    o_ref[0] = jnp.where(head_a, out[2], out[4]).astype(BF16)


def _sb_attention(sb, *, tile=256):
    b, t, _ = sb.shape
    n_pairs = N_SB_HEADS * HEAD_DIM // LANES
    assert t % tile == 0
    kv = lambda part: pl.BlockSpec((1, t, LANES),
                                   lambda bb, jp, i: (bb, 0, part * n_pairs + jp))
    return pl.pallas_call(
        functools.partial(_sb_kernel, tile=tile),
        grid=(b, n_pairs, t // tile),
        in_specs=[pl.BlockSpec((1, tile, LANES), lambda bb, jp, i: (bb, i, jp)),
                  kv(1), kv(2)],
        out_specs=pl.BlockSpec((1, tile, LANES), lambda bb, jp, i: (bb, i, jp)),
        out_shape=jax.ShapeDtypeStruct((b, t, n_pairs * LANES), BF16),
        compiler_params=pltpu.CompilerParams(
            dimension_semantics=("parallel", "parallel", "arbitrary"),
            vmem_limit_bytes=VMEM_LIMIT),
        name="stick_breaking_attn",
    )(sb, sb, sb)


def _out_kernel(x_ref, gain_ref, od_ref, os_ref, wgd_ref, wgs_ref, wpd_ref, wps_ref,
                wo_ref, o_ref):
    x = x_ref[...]
    h = _rms(x, gain_ref[...]).astype(BF16)
    y = _sigmoid(jnp.dot(h, wgd_ref[...], preferred_element_type=F32)) * jnp.dot(
        od_ref[...], wpd_ref[...], preferred_element_type=F32)
    y = y + _sigmoid(jnp.dot(h, wgs_ref[...], preferred_element_type=F32)) * jnp.dot(
        os_ref[...], wps_ref[...], preferred_element_type=F32)
    o_ref[...] = x + jnp.dot(y.astype(BF16), wo_ref[...], preferred_element_type=F32)


def _mix_out(x2, gain, o_dil, o_sb, wgd, wgs, wpd, wps, wo, *, tm=512):
    n, d = x2.shape
    row = lambda w: pl.BlockSpec((tm, w), lambda i: (i, 0))
    return pl.pallas_call(
        _out_kernel,
        grid=(n // tm,),
        in_specs=[row(d), _resident((1, d)), row(o_dil.shape[1]), row(o_sb.shape[1]),
                  _resident(wgd.shape), _resident(wgs.shape), _resident(wpd.shape),
                  _resident(wps.shape), _resident(wo.shape)],
        out_specs=row(d),
        out_shape=jax.ShapeDtypeStruct((n, d), F32),
        compiler_params=pltpu.CompilerParams(
            dimension_semantics=("parallel",), vmem_limit_bytes=VMEM_LIMIT),
        name="mix_out",
    )(x2, gain, o_dil, o_sb, wgd, wgs, wpd, wps, wo)


def _rope_tables(t, tm):
    half = ROPE_DIM // 2
    inv_freq = ROPE_THETA ** (-jnp.arange(0, ROPE_DIM, 2, dtype=F32) / ROPE_DIM)
    lane = jnp.arange(LANES) % HEAD_DIM
    tabs = []
    for d in DILATIONS:
        pos = jnp.arange(t, dtype=jnp.int32).reshape(t // tm, tm // d, d)
        pos = pos.transpose(0, 2, 1).reshape(t).astype(F32)
        ang = pos[:, None] * inv_freq[None, :]
        cos = jnp.cos(ang)[:, lane % half]
        sin = jnp.sin(ang)[:, lane % half]
        c = jnp.where(lane < ROPE_DIM, cos, 1.0)
        s1 = jnp.where(lane < half, -sin, 0.0)
        s2 = jnp.where((lane >= half) & (lane < ROPE_DIM), sin, 0.0)
        tabs.append(jnp.stack([c, s1, s2]).astype(F32))
    return tabs


def kernel(x, norm_ffn1, ffn1_w_gate, ffn1_w_up, ffn1_w_down, norm_mix, w_in,
           w_proj_dil, w_proj_sb, w_out, norm_ffn2, ffn2_w_gate, ffn2_w_up,
           ffn2_w_down, norm_final):
    b, t, dm = x.shape
    depth = w_in.shape[0]
    tm_qkv = 512
    tabs = _rope_tables(t, tm_qkv)
    d_dil = len(DILATIONS) * HEADS_PER_GROUP * HEAD_DIM
    gw = HEADS_PER_GROUP * HEAD_DIM
    o1 = 3 * d_dil
    o2 = o1 + 3 * N_SB_HEADS * HEAD_DIM
    o3 = o2 + dm
    bf = lambda a: a.astype(BF16)
    fin = norm_final.reshape(1, dm)

    def group_cols(w, g):
        return jnp.concatenate(
            [w[:, p * d_dil + g * gw: p * d_dil + (g + 1) * gw] for p in range(3)], axis=1)

    x2 = x.reshape(b * t, dm)
    for l in range(depth):
        x2 = _ffn(x2, norm_ffn1[l].reshape(1, dm), bf(ffn1_w_gate[l]), bf(ffn1_w_up[l]),
                  bf(ffn1_w_down[l]), fin, final=False)
        wl = w_in[l]
        wn = bf(jnp.concatenate([group_cols(wl, 0), wl[:, o1:o2]], axis=1))
        g0, g1, g2, sb = _qkv(x2.reshape(b, t, dm), norm_mix[l].reshape(1, dm), wn,
                              bf(group_cols(wl, 1)), bf(group_cols(wl, 2)), tabs, tm=tm_qkv)
        o_dil = _dil_attention(g0, g1, g2)
        o_sb = _sb_attention(sb)
        x2 = _mix_out(x2, norm_mix[l].reshape(1, dm), o_dil.reshape(b * t, -1),
                      o_sb.reshape(b * t, -1), bf(wl[:, o2:o3]), bf(wl[:, o3:]),
                      bf(w_proj_dil[l]), bf(w_proj_sb[l]), bf(w_out[l]))
        x2 = _ffn(x2, norm_ffn2[l].reshape(1, dm), bf(ffn2_w_gate[l]), bf(ffn2_w_up[l]),
                  bf(ffn2_w_down[l]), fin, final=(l == depth - 1))
    return x2.reshape(b, t, dm)
```

```python
import functools

import jax
import jax.numpy as jnp
from jax import lax
from jax.experimental import pallas as pl
from jax.experimental.pallas import tpu as pltpu

F32 = jnp.float32
BF16 = jnp.bfloat16

HEAD_DIM = 64
DILATIONS = (1, 4, 16)
SPAN = 128
HEADS_PER_GROUP = 4
N_SB_HEADS = 4
ROPE_DIM = HEAD_DIM // 4
ROPE_THETA = 500000.0
RMS_EPS = 1e-6
LANES = 128
PART_W = HEADS_PER_GROUP * HEAD_DIM
GROUP_W = 3 * PART_W
D_DIL = len(DILATIONS) * PART_W
QK_SCALE = HEAD_DIM ** -0.5

ROW_TILE = 512
FF_CHUNK = 256
OUT_CHUNK = 256
SB_TILE = 256
DIL_UNROLL = 8
MERGE_ROWS = 256
CAST_BLOCK_BYTES = 4 << 20
SB_EXIT = 105.0

VMEM_LIMIT = 56 * 1024 * 1024


def _rms(x, gain):
    ms = jnp.mean(x * x, axis=-1, keepdims=True)
    return x * lax.rsqrt(ms + RMS_EPS) * gain


def _sigmoid(x):
    return 1.0 / (1.0 + jnp.exp(-x))


def _nt_dot(a, b):
    return lax.dot_general(a, b, (((1,), (1,)), ((), ())), preferred_element_type=F32)


def _layer(arr, l):
    return pl.BlockSpec((None,) + arr.shape[1:], lambda *_: (l, 0, 0),
                        pipeline_mode=pl.Buffered(1))


def _params(*semantics):
    return pltpu.CompilerParams(dimension_semantics=semantics, vmem_limit_bytes=VMEM_LIMIT)


def _cast_kernel(w_ref, o_ref):
    o_ref[...] = w_ref[...].astype(BF16)


def _to_bf16(w):
    depth, r, c = w.shape
    rows = depth * r
    tr = 1 << (min(rows, CAST_BLOCK_BYTES // (4 * c)).bit_length() - 1)
    while rows % tr:
        tr //= 2
    assert tr >= 16
    spec = pl.BlockSpec((tr, c), lambda i: (i, 0))
    out = pl.pallas_call(
        _cast_kernel, grid=(rows // tr,), in_specs=[spec], out_specs=spec,
        out_shape=jax.ShapeDtypeStruct((rows, c), BF16),
        compiler_params=_params("parallel"), name="cast_bf16",
    )(w.reshape(rows, c))
    return out.reshape(depth, r, c)


def _ffn_kernel(x_ref, gain_ref, wg_ref, wu_ref, wd_ref, fin_ref, o_ref, *, final):
    x = x_ref[...]
    h = _rms(x, gain_ref[...]).astype(BF16)
    acc = jnp.zeros(x.shape, F32)
    for c in range(wg_ref.shape[1] // FF_CHUNK):
        cols = slice(c * FF_CHUNK, (c + 1) * FF_CHUNK)
        g = jnp.dot(h, wg_ref[:, cols], preferred_element_type=F32)
        u = jnp.dot(h, wu_ref[:, cols], preferred_element_type=F32)
        a = (g * _sigmoid(g) * u).astype(BF16)
        acc = acc + jnp.dot(a, wd_ref[cols, :], preferred_element_type=F32)
    y = x + 0.5 * acc
    if final:
        y = _rms(y, fin_ref[...])
    o_ref[...] = y


def _ffn(x2, gains, wg, wu, wd, fin_gain, l, *, final):
    n, d = x2.shape
    assert n % ROW_TILE == 0 and wg.shape[2] % FF_CHUNK == 0
    row = pl.BlockSpec((ROW_TILE, d), lambda i: (i, 0))
    return pl.pallas_call(
        functools.partial(_ffn_kernel, final=final),
        grid=(n // ROW_TILE,),
        in_specs=[row, _layer(gains, l), _layer(wg, l), _layer(wu, l), _layer(wd, l),
                  _layer(fin_gain, 0)],
        out_specs=row,
        out_shape=jax.ShapeDtypeStruct((n, d), F32),
        compiler_params=_params("parallel"),
        name="ffn_final" if final else "ffn",
    )(x2, gains, wg, wu, wd, fin_gain)


def _rope(xs, tab_ref):
    return (xs * tab_ref[0]
            + pltpu.roll(xs, LANES - ROPE_DIM // 2, 1) * tab_ref[1]
            + pltpu.roll(xs, ROPE_DIM // 2, 1) * tab_ref[2])


def _project(hb, w_ref, col0, part_stride, tab_ref, out_ref, d):
    rm = hb.shape[0] // (d or 1)
    for part in range(3):
        c0 = col0 + part * part_stride
        p = jnp.dot(hb, w_ref[:, c0:c0 + PART_W], preferred_element_type=F32)
        for half in range(PART_W // LANES):
            xs = p[:, half * LANES:(half + 1) * LANES]
            if part < 2 and tab_ref is not None:
                xs = _rope(xs, tab_ref)
            if part == 0:
                xs = xs * QK_SCALE
            xb = xs.astype(BF16)
            cols = slice(part * PART_W + half * LANES, part * PART_W + (half + 1) * LANES)
            if d is None:
                out_ref[0, :, cols] = xb
            else:
                for r in range(d):
                    out_ref[0, r, :, cols] = xb[r * rm:(r + 1) * rm, :]


def _qkv_kernel(x_ref, gain_ref, w_ref, tab0_ref, tab1_ref, tab2_ref,
                g0_ref, g1_ref, g2_ref, sb_ref, h_scr, hp1_scr, hp2_scr):
    tm = x_ref.shape[1]
    h = _rms(x_ref[0], gain_ref[...])
    n_slab = h.shape[1] // LANES
    for c in range(n_slab):
        h_scr[c] = h[:, c * LANES:(c + 1) * LANES]
    hn = h.astype(BF16)
    _project(hn, w_ref, 0, D_DIL, tab0_ref, g0_ref, 1)
    _project(hn, w_ref, 3 * D_DIL, PART_W, None, sb_ref, None)
    for g, tab_ref, out_ref, hp_scr in ((1, tab1_ref, g1_ref, hp1_scr),
                                        (2, tab2_ref, g2_ref, hp2_scr)):
        d = DILATIONS[g]
        rm = tm // d
        for r in range(d):
            for c in range(n_slab):
                hp_scr[r * rm:(r + 1) * rm, c * LANES:(c + 1) * LANES] = (
                    h_scr[c, pl.ds(r, rm, stride=d), :].astype(BF16))
        _project(hp_scr[...], w_ref, g * PART_W, D_DIL, tab_ref, out_ref, d)


def _qkv(x, gains, w_in, tabs, l):
    b, t, dm = x.shape
    tm = ROW_TILE
    d1, d2 = DILATIONS[1], DILATIONS[2]
    assert t % tm == 0 and tm % (16 * d2) == 0
    tab_spec = pl.BlockSpec((3, tm, LANES), lambda i, bb: (0, i, 0))
    return pl.pallas_call(
        _qkv_kernel,
        grid=(t // tm, b),
        in_specs=[pl.BlockSpec((1, tm, dm), lambda i, bb: (bb, i, 0)),
                  _layer(gains, l), _layer(w_in, l), tab_spec, tab_spec, tab_spec],
        out_specs=[pl.BlockSpec((1, 1, tm, GROUP_W), lambda i, bb: (bb, 0, i, 0)),
                   pl.BlockSpec((1, d1, tm // d1, GROUP_W), lambda i, bb: (bb, 0, i, 0)),
                   pl.BlockSpec((1, d2, tm // d2, GROUP_W), lambda i, bb: (bb, 0, i, 0)),
                   pl.BlockSpec((1, tm, GROUP_W), lambda i, bb: (bb, i, 0))],
        out_shape=[jax.ShapeDtypeStruct((b, 1, t, GROUP_W), BF16),
                   jax.ShapeDtypeStruct((b, d1, t // d1, GROUP_W), BF16),
                   jax.ShapeDtypeStruct((b, d2, t // d2, GROUP_W), BF16),
                   jax.ShapeDtypeStruct((b, t, GROUP_W), BF16)],
        scratch_shapes=[pltpu.VMEM((dm // LANES, tm, LANES), F32),
                        pltpu.VMEM((tm, dm), BF16), pltpu.VMEM((tm, dm), BF16)],
        compiler_params=_params("parallel", "parallel"),
        name="qkv_proj",
    )(x, gains, w_in, *tabs)


def _dil_kernel(q0, k0, v0, q1, k1, v1, q2, k2, v2, o_ref, o_scr, l_scr):
    t = o_ref.shape[1]
    lane = lax.broadcasted_iota(jnp.int32, (1, LANES), 1)
    mask_a = (lane < HEAD_DIM).astype(BF16)
    mask_b = (lane >= HEAD_DIM).astype(BF16)
    head_a = lane < HEAD_DIM
    dist = (lax.broadcasted_iota(jnp.int32, (SPAN, 2 * SPAN), 0)
            - lax.broadcasted_iota(jnp.int32, (SPAN, 2 * SPAN), 1))
    bias_first = jnp.where(dist >= 0, 0.0, -1e30)
    bias_band = jnp.where((dist + SPAN >= 0) & (dist <= 0), 0.0, -1e30)

    for g, (q_ref, k_ref, v_ref) in enumerate(((q0, k0, v0), (q1, k1, v1), (q2, k2, v2))):
        d = DILATIONS[g]
        nb = t // d // SPAN

        def block(idx, carry, g=g, d=d, nb=nb, q_ref=q_ref, k_ref=k_ref, v_ref=v_ref):
            r = idx // nb
            n = idx % nb
            ws = jnp.maximum(n - 1, 0) * SPAN
            q = q_ref[0, r, pl.ds(pl.multiple_of(n * SPAN, SPAN), SPAN), :]
            kw = k_ref[0, r, pl.ds(pl.multiple_of(ws, SPAN), 2 * SPAN), :]
            vw = v_ref[0, r, pl.ds(pl.multiple_of(ws, SPAN), 2 * SPAN), :]
            bias = jnp.where(n == 0, bias_first, bias_band)
            outs, lses = [], []
            for m_h in (mask_a, mask_b):
                s = _nt_dot(q * m_h, kw) + bias
                m = jnp.max(s, axis=1, keepdims=True)
                p = jnp.exp(s - m)
                den = jnp.sum(p, axis=1, keepdims=True)
                o = jnp.dot(p.astype(BF16), vw, preferred_element_type=F32) / den
                outs.append(o)
                lses.append(jnp.broadcast_to(m + jnp.log(den), (SPAN, LANES)))
            o = jnp.where(head_a, outs[0], outs[1])
            lse = jnp.where(head_a, lses[0], lses[1])
            start = r + d * n * SPAN
            rows = pl.ds(start, SPAN) if d == 1 else pl.ds(start, SPAN, stride=d)
            o_scr[g, rows, :] = o
            l_scr[g, rows, :] = lse
            return carry

        lax.fori_loop(0, d * nb, block, 0, unroll=DIL_UNROLL)

    def merge(c, carry):
        rows = pl.ds(pl.multiple_of(c * MERGE_ROWS, MERGE_ROWS), MERGE_ROWS)
        l0, l1, l2 = l_scr[0, rows, :], l_scr[1, rows, :], l_scr[2, rows, :]
        m = jnp.maximum(jnp.maximum(l0, l1), l2)
        e0, e1, e2 = jnp.exp(l0 - m), jnp.exp(l1 - m), jnp.exp(l2 - m)
        num = e0 * o_scr[0, rows, :] + e1 * o_scr[1, rows, :] + e2 * o_scr[2, rows, :]
        o_ref[0, rows, :] = (num / (e0 + e1 + e2)).astype(BF16)
        return carry

    lax.fori_loop(0, t // MERGE_ROWS, merge, 0)


def _dil_attention(g0, g1, g2):
    b, _, t, _ = g0.shape
    assert t % (2 * SPAN * DILATIONS[-1]) == 0 and t % MERGE_ROWS == 0
    n_pairs = PART_W // LANES
    in_specs, args = [], []
    for arr in (g0, g1, g2):
        d, ln = arr.shape[1], arr.shape[2]
        for part in range(3):
            in_specs.append(pl.BlockSpec(
                (1, d, ln, LANES),
                lambda bb, jp, part=part: (bb, 0, 0, part * n_pairs + jp)))
            args.append(arr)
    return pl.pallas_call(
        _dil_kernel,
        grid=(b, n_pairs),
        in_specs=in_specs,
        out_specs=pl.BlockSpec((1, t, LANES), lambda bb, jp: (bb, 0, jp)),
        out_shape=jax.ShapeDtypeStruct((b, t, n_pairs * LANES), BF16),
        scratch_shapes=[pltpu.VMEM((3, t, LANES), F32), pltpu.VMEM((3, t, LANES), F32)],
        compiler_params=_params("parallel", "parallel"),
        name="dilated_attn",
    )(*args)


def _sb_kernel(q_ref, k_ref, v_ref, o_ref):
    tile = SB_TILE
    i = pl.program_id(2)
    lane = lax.broadcasted_iota(jnp.int32, (1, LANES), 1)
    head_a = lane < HEAD_DIM
    q = q_ref[0]
    q_heads = (q * head_a.astype(BF16), q * (lane >= HEAD_DIM).astype(BF16))
    row = lax.broadcasted_iota(jnp.int32, (tile, tile), 0)
    col = lax.broadcasted_iota(jnp.int32, (tile, tile), 1)
    tri = (row >= col).astype(BF16)
    tri2 = jnp.concatenate([tri, tri], axis=0)
    past = col < row

    def scores(h, j, keep):
        k = k_ref[0, pl.ds(pl.multiple_of(j * tile, tile), tile), :]
        z = _nt_dot(q_heads[h], k)
        sp = jnp.maximum(z, 0.0) + jnp.log(1.0 + jnp.exp(-jnp.abs(z)))
        if keep is not None:
            sp = jnp.where(keep, sp, 0.0)
        return z, sp

    def weighted(z, sp, j, run, keep):
        v = v_ref[0, pl.ds(pl.multiple_of(j * tile, tile), tile), :]
        hi = sp.astype(BF16)
        lo = (sp - hi.astype(F32)).astype(BF16)
        csum = jnp.dot(jnp.concatenate([hi, lo], axis=1), tri2, preferred_element_type=F32)
        w = jnp.exp(z - csum - run)
        if keep is not None:
            w = jnp.where(keep, w, 0.0)
        return jnp.dot(w.astype(BF16), v, preferred_element_type=F32)

    def live(runs):
        low = jnp.minimum(jnp.min(runs[0]), jnp.min(runs[1]))
        return (low < SB_EXIT).astype(jnp.int32)

    has_prev = i > 0
    j_prev = jnp.maximum(i - 1, 0)
    accs, runs = [], []
    for h in range(2):
        z0, sp0 = scores(h, i, past)
        z1, sp1 = scores(h, j_prev, has_prev)
        run0 = jnp.sum(sp0, axis=1, keepdims=True)
        accs.append(weighted(z0, sp0, i, 0.0, past)
                    + weighted(z1, sp1, j_prev, run0, has_prev))
        runs.append(run0 + jnp.sum(sp1, axis=1, keepdims=True))

    def step(c):
        j = i - 2 - c[0]
        new_acc, new_run = [], []
        for h in range(2):
            z, sp = scores(h, j, None)
            new_acc.append(c[2 + h] + weighted(z, sp, j, c[4 + h], None))
            new_run.append(c[4 + h] + jnp.sum(sp, axis=1, keepdims=True))
        return (c[0] + 1, live(new_run), *new_acc, *new_run)

    out = lax.while_loop(lambda c: (c[0] < i - 1) & (c[1] > 0), step,
                         (jnp.int32(0), live(runs), *accs, *runs))
    o_ref[0] = jnp.where(head_a, out[2], out[3]).astype(BF16)


def _sb_attention(sb):
    b, t, _ = sb.shape
    n_pairs = N_SB_HEADS * HEAD_DIM // LANES
    assert t % SB_TILE == 0
    kv = lambda part: pl.BlockSpec((1, t, LANES),
                                   lambda bb, jp, i: (bb, 0, part * n_pairs + jp))
    q_tile = pl.BlockSpec((1, SB_TILE, LANES), lambda bb, jp, i: (bb, i, jp))
    return pl.pallas_call(
        _sb_kernel,
        grid=(b, n_pairs, t // SB_TILE),
        in_specs=[q_tile, kv(1), kv(2)],
        out_specs=q_tile,
        out_shape=jax.ShapeDtypeStruct((b, t, n_pairs * LANES), BF16),
        compiler_params=_params("parallel", "parallel", "arbitrary"),
        name="stick_breaking_attn",
    )(sb, sb, sb)


def _out_kernel(x_ref, gain_ref, od_ref, os_ref, win_ref, wpd_ref, wps_ref, wo_ref, o_ref,
                *, gate_col0):
    x = x_ref[...]
    dm = x.shape[1]
    h = _rms(x, gain_ref[...]).astype(BF16)
    od, osb = od_ref[...], os_ref[...]
    dot = functools.partial(jnp.dot, preferred_element_type=F32)
    acc = jnp.zeros(x.shape, F32)
    for c in range(dm // OUT_CHUNK):
        cols = slice(c * OUT_CHUNK, (c + 1) * OUT_CHUNK)
        gd = slice(gate_col0 + c * OUT_CHUNK, gate_col0 + (c + 1) * OUT_CHUNK)
        gs = slice(gate_col0 + dm + c * OUT_CHUNK, gate_col0 + dm + (c + 1) * OUT_CHUNK)
        y = (_sigmoid(dot(h, win_ref[:, gd])) * dot(od, wpd_ref[:, cols])
             + _sigmoid(dot(h, win_ref[:, gs])) * dot(osb, wps_ref[:, cols]))
        acc = acc + dot(y.astype(BF16), wo_ref[cols, :])
    o_ref[...] = x + acc


def _mix_out(x2, gains, o_dil, o_sb, w_in, wpd, wps, wo, l):
    n, d = x2.shape
    assert n % ROW_TILE == 0 and d % OUT_CHUNK == 0
    row = lambda w: pl.BlockSpec((ROW_TILE, w), lambda i: (i, 0))
    return pl.pallas_call(
        functools.partial(_out_kernel, gate_col0=w_in.shape[2] - 2 * d),
        grid=(n // ROW_TILE,),
        in_specs=[row(d), _layer(gains, l), row(o_dil.shape[1]), row(o_sb.shape[1]),
                  _layer(w_in, l), _layer(wpd, l), _layer(wps, l), _layer(wo, l)],
        out_specs=row(d),
        out_shape=jax.ShapeDtypeStruct((n, d), F32),
        compiler_params=_params("parallel"),
        name="mix_out",
    )(x2, gains, o_dil, o_sb, w_in, wpd, wps, wo)


def _rope_tables(t, tm):
    half = ROPE_DIM // 2
    inv_freq = ROPE_THETA ** (-jnp.arange(0, ROPE_DIM, 2, dtype=F32) / ROPE_DIM)
    lane = jnp.arange(LANES) % HEAD_DIM
    tabs = []
    for d in DILATIONS:
        pos = jnp.arange(t, dtype=jnp.int32).reshape(t // tm, tm // d, d)
        pos = pos.transpose(0, 2, 1).reshape(t).astype(F32)
        ang = pos[:, None] * inv_freq[None, :]
        cos = jnp.cos(ang)[:, lane % half]
        sin = jnp.sin(ang)[:, lane % half]
        c = jnp.where(lane < ROPE_DIM, cos, 1.0)
        s1 = jnp.where(lane < half, -sin, 0.0)
        s2 = jnp.where((lane >= half) & (lane < ROPE_DIM), sin, 0.0)
        tabs.append(jnp.stack([c, s1, s2]).astype(F32))
    return tabs


def kernel(x, norm_ffn1, ffn1_w_gate, ffn1_w_up, ffn1_w_down, norm_mix, w_in,
           w_proj_dil, w_proj_sb, w_out, norm_ffn2, ffn2_w_gate, ffn2_w_up,
           ffn2_w_down, norm_final):
    b, t, dm = x.shape
    depth = w_in.shape[0]
    assert w_in.shape[2] == 3 * D_DIL + 3 * N_SB_HEADS * HEAD_DIM + 2 * dm
    tabs = _rope_tables(t, ROW_TILE)
    gain3 = lambda g: g.reshape(-1, 1, dm)
    n1, nm, n2, fin = gain3(norm_ffn1), gain3(norm_mix), gain3(norm_ffn2), gain3(norm_final)
    f1g, f1u, f1d = _to_bf16(ffn1_w_gate), _to_bf16(ffn1_w_up), _to_bf16(ffn1_w_down)
    f2g, f2u, f2d = _to_bf16(ffn2_w_gate), _to_bf16(ffn2_w_up), _to_bf16(ffn2_w_down)
    win, wpd, wps, wo = (_to_bf16(w_in), _to_bf16(w_proj_dil), _to_bf16(w_proj_sb),
                         _to_bf16(w_out))

    x2 = x.reshape(b * t, dm)
    for l in range(depth):
        x2 = _ffn(x2, n1, f1g, f1u, f1d, fin, l, final=False)
        g0, g1, g2, sb = _qkv(x2.reshape(b, t, dm), nm, win, tabs, l)
        o_dil = _dil_attention(g0, g1, g2)
        o_sb = _sb_attention(sb)
        x2 = _mix_out(x2, nm, o_dil.reshape(b * t, -1), o_sb.reshape(b * t, -1),
                      win, wpd, wps, wo, l)
        x2 = _ffn(x2, n2, f2g, f2u, f2d, fin, l, final=(l == depth - 1))
    return x2.reshape(b, t, dm)
```

```python
import functools

import jax
import jax.numpy as jnp
from jax import lax
from jax.experimental import pallas as pl
from jax.experimental.pallas import tpu as pltpu

F32 = jnp.float32
BF16 = jnp.bfloat16

HEAD_DIM = 64
DILATIONS = (1, 4, 16)
SPAN = 128
HEADS_PER_GROUP = 4
N_SB_HEADS = 4
ROPE_DIM = HEAD_DIM // 4
ROPE_THETA = 500000.0
RMS_EPS = 1e-6
LANES = 128
PART_W = HEADS_PER_GROUP * HEAD_DIM
GROUP_W = 3 * PART_W
D_DIL = len(DILATIONS) * PART_W
QK_SCALE = HEAD_DIM ** -0.5

ROW_TILE = 512
FFN_ROWS = 1024
FF_CHUNK = 256
OUT_CHUNK = 256
SB_TILE = 256
SB_QTILES = 2
DIL_UNROLL = 8
MERGE_ROWS = 256
CAST_BLOCK_BYTES = 4 << 20
SB_EXIT = 105.0

VMEM_LIMIT = 56 * 1024 * 1024


def _rms(x, gain):
    ms = jnp.mean(x * x, axis=-1, keepdims=True)
    return x * lax.rsqrt(ms + RMS_EPS) * gain


def _sigmoid(x):
    return 1.0 / (1.0 + jnp.exp(-x))


def _nt_dot(a, b):
    return lax.dot_general(a, b, (((1,), (1,)), ((), ())), preferred_element_type=F32)


def _layer(arr, l):
    return pl.BlockSpec((None,) + arr.shape[1:], lambda *_: (l, 0, 0),
                        pipeline_mode=pl.Buffered(1))


def _params(*semantics):
    return pltpu.CompilerParams(dimension_semantics=semantics, vmem_limit_bytes=VMEM_LIMIT)


def _cast_kernel(w_ref, o_ref):
    o_ref[...] = w_ref[...].astype(BF16)


def _to_bf16(w):
    depth, r, c = w.shape
    rows = depth * r
    tr = 1 << (min(rows, CAST_BLOCK_BYTES // (4 * c)).bit_length() - 1)
    while rows % tr:
        tr //= 2
    assert tr >= 16
    spec = pl.BlockSpec((tr, c), lambda i: (i, 0))
    out = pl.pallas_call(
        _cast_kernel, grid=(rows // tr,), in_specs=[spec], out_specs=spec,
        out_shape=jax.ShapeDtypeStruct((rows, c), BF16),
        compiler_params=_params("parallel"), name="cast_bf16",
    )(w.reshape(rows, c))
    return out.reshape(depth, r, c)


def _ffn_kernel(x_ref, gain_ref, wg_ref, wu_ref, wd_ref, fin_ref, o_ref, *, final):
    x = x_ref[...]
    h = _rms(x, gain_ref[...]).astype(BF16)
    acc = jnp.zeros(x.shape, F32)
    for c in range(wg_ref.shape[1] // FF_CHUNK):
        cols = slice(c * FF_CHUNK, (c + 1) * FF_CHUNK)
        g = jnp.dot(h, wg_ref[:, cols], preferred_element_type=F32)
        u = jnp.dot(h, wu_ref[:, cols], preferred_element_type=F32)
        a = (g * _sigmoid(g) * u).astype(BF16)
        acc = acc + jnp.dot(a, wd_ref[cols, :], preferred_element_type=F32)
    y = x + 0.5 * acc
    if final:
        y = _rms(y, fin_ref[...])
    o_ref[...] = y


def _ffn(x2, gains, wg, wu, wd, fin_gain, l, *, final):
    n, d = x2.shape
    assert n % FFN_ROWS == 0 and wg.shape[2] % FF_CHUNK == 0
    row = pl.BlockSpec((FFN_ROWS, d), lambda i: (i, 0))
    return pl.pallas_call(
        functools.partial(_ffn_kernel, final=final),
        grid=(n // FFN_ROWS,),
        in_specs=[row, _layer(gains, l), _layer(wg, l), _layer(wu, l), _layer(wd, l),
                  _layer(fin_gain, 0)],
        out_specs=row,
        out_shape=jax.ShapeDtypeStruct((n, d), F32),
        compiler_params=_params("parallel"),
        name="ffn_final" if final else "ffn",
    )(x2, gains, wg, wu, wd, fin_gain)


def _rope(xs, tab_ref):
    return (xs * tab_ref[0]
            + pltpu.roll(xs, LANES - ROPE_DIM // 2, 1) * tab_ref[1]
            + pltpu.roll(xs, ROPE_DIM // 2, 1) * tab_ref[2])


def _project(hb, w_ref, col0, part_stride, tab_ref, out_ref, d):
    rm = hb.shape[0] // (d or 1)
    for part in range(3):
        c0 = col0 + part * part_stride
        p = jnp.dot(hb, w_ref[:, c0:c0 + PART_W], preferred_element_type=F32)
        for half in range(PART_W // LANES):
            xs = p[:, half * LANES:(half + 1) * LANES]
            if part < 2 and tab_ref is not None:
                xs = _rope(xs, tab_ref)
            if part == 0:
                xs = xs * QK_SCALE
            xb = xs.astype(BF16)
            cols = slice(part * PART_W + half * LANES, part * PART_W + (half + 1) * LANES)
            if d is None:
                out_ref[0, :, cols] = xb
            else:
                for r in range(d):
                    out_ref[0, r, :, cols] = xb[r * rm:(r + 1) * rm, :]


def _qkv_kernel(x_ref, gain_ref, w_ref, tab0_ref, tab1_ref, tab2_ref,
                g0_ref, g1_ref, g2_ref, sb_ref, h_scr, hp1_scr, hp2_scr):
    tm = x_ref.shape[1]
    h = _rms(x_ref[0], gain_ref[...])
    n_slab = h.shape[1] // LANES
    for c in range(n_slab):
        h_scr[c] = h[:, c * LANES:(c + 1) * LANES]
    hn = h.astype(BF16)
    _project(hn, w_ref, 0, D_DIL, tab0_ref, g0_ref, 1)
    _project(hn, w_ref, 3 * D_DIL, PART_W, None, sb_ref, None)
    for g, tab_ref, out_ref, hp_scr in ((1, tab1_ref, g1_ref, hp1_scr),
                                        (2, tab2_ref, g2_ref, hp2_scr)):
        d = DILATIONS[g]
        rm = tm // d
        for r in range(d):
            for c in range(n_slab):
                hp_scr[r * rm:(r + 1) * rm, c * LANES:(c + 1) * LANES] = (
                    h_scr[c, pl.ds(r, rm, stride=d), :].astype(BF16))
        _project(hp_scr[...], w_ref, g * PART_W, D_DIL, tab_ref, out_ref, d)


def _qkv(x, gains, w_in, tabs, l):
    b, t, dm = x.shape
    tm = ROW_TILE
    d1, d2 = DILATIONS[1], DILATIONS[2]
    assert t % tm == 0 and tm % (16 * d2) == 0
    tab_spec = pl.BlockSpec((3, tm, LANES), lambda i, bb: (0, i, 0))
    return pl.pallas_call(
        _qkv_kernel,
        grid=(t // tm, b),
        in_specs=[pl.BlockSpec((1, tm, dm), lambda i, bb: (bb, i, 0)),
                  _layer(gains, l), _layer(w_in, l), tab_spec, tab_spec, tab_spec],
        out_specs=[pl.BlockSpec((1, 1, tm, GROUP_W), lambda i, bb: (bb, 0, i, 0)),
                   pl.BlockSpec((1, d1, tm // d1, GROUP_W), lambda i, bb: (bb, 0, i, 0)),
                   pl.BlockSpec((1, d2, tm // d2, GROUP_W), lambda i, bb: (bb, 0, i, 0)),
                   pl.BlockSpec((1, tm, GROUP_W), lambda i, bb: (bb, i, 0))],
        out_shape=[jax.ShapeDtypeStruct((b, 1, t, GROUP_W), BF16),
                   jax.ShapeDtypeStruct((b, d1, t // d1, GROUP_W), BF16),
                   jax.ShapeDtypeStruct((b, d2, t // d2, GROUP_W), BF16),
                   jax.ShapeDtypeStruct((b, t, GROUP_W), BF16)],
        scratch_shapes=[pltpu.VMEM((dm // LANES, tm, LANES), F32),
                        pltpu.VMEM((tm, dm), BF16), pltpu.VMEM((tm, dm), BF16)],
        compiler_params=_params("parallel", "parallel"),
        name="qkv_proj",
    )(x, gains, w_in, *tabs)


def _dil_kernel(q0, k0, v0, q1, k1, v1, q2, k2, v2, o_ref, o_scr, l_scr):
    t = o_ref.shape[1]
    lane = lax.broadcasted_iota(jnp.int32, (1, LANES), 1)
    mask_a = (lane < HEAD_DIM).astype(BF16)
    mask_b = (lane >= HEAD_DIM).astype(BF16)
    head_a = lane < HEAD_DIM
    dist = (lax.broadcasted_iota(jnp.int32, (SPAN, 2 * SPAN), 0)
            - lax.broadcasted_iota(jnp.int32, (SPAN, 2 * SPAN), 1))
    bias_first = jnp.where(dist >= 0, 0.0, -1e30)
    bias_band = jnp.where((dist + SPAN >= 0) & (dist <= 0), 0.0, -1e30)

    for g, (q_ref, k_ref, v_ref) in enumerate(((q0, k0, v0), (q1, k1, v1), (q2, k2, v2))):
        d = DILATIONS[g]
        nb = t // d // SPAN

        def block(idx, carry, g=g, d=d, nb=nb, q_ref=q_ref, k_ref=k_ref, v_ref=v_ref):
            r = idx // nb
            n = idx % nb
            ws = jnp.maximum(n - 1, 0) * SPAN
            q = q_ref[0, r, pl.ds(pl.multiple_of(n * SPAN, SPAN), SPAN), :]
            kw = k_ref[0, r, pl.ds(pl.multiple_of(ws, SPAN), 2 * SPAN), :]
            vw = v_ref[0, r, pl.ds(pl.multiple_of(ws, SPAN), 2 * SPAN), :]
            bias = jnp.where(n == 0, bias_first, bias_band)
            outs, lses = [], []
            for m_h in (mask_a, mask_b):
                s = _nt_dot(q * m_h, kw) + bias
                m = jnp.max(s, axis=1, keepdims=True)
                p = jnp.exp(s - m)
                den = jnp.sum(p, axis=1, keepdims=True)
                o = jnp.dot(p.astype(BF16), vw, preferred_element_type=F32) / den
                outs.append(o)
                lses.append(jnp.broadcast_to(m + jnp.log(den), (SPAN, LANES)))
            o = jnp.where(head_a, outs[0], outs[1])
            lse = jnp.where(head_a, lses[0], lses[1])
            start = r + d * n * SPAN
            rows = pl.ds(start, SPAN) if d == 1 else pl.ds(start, SPAN, stride=d)
            o_scr[g, rows, :] = o
            l_scr[g, rows, :] = lse
            return carry

        lax.fori_loop(0, d * nb, block, 0, unroll=DIL_UNROLL)

    def merge(c, carry):
        rows = pl.ds(pl.multiple_of(c * MERGE_ROWS, MERGE_ROWS), MERGE_ROWS)
        l0, l1, l2 = l_scr[0, rows, :], l_scr[1, rows, :], l_scr[2, rows, :]
        m = jnp.maximum(jnp.maximum(l0, l1), l2)
        e0, e1, e2 = jnp.exp(l0 - m), jnp.exp(l1 - m), jnp.exp(l2 - m)
        num = e0 * o_scr[0, rows, :] + e1 * o_scr[1, rows, :] + e2 * o_scr[2, rows, :]
        o_ref[0, rows, :] = (num / (e0 + e1 + e2)).astype(BF16)
        return carry

    lax.fori_loop(0, t // MERGE_ROWS, merge, 0)


def _dil_attention(g0, g1, g2):
    b, _, t, _ = g0.shape
    assert t % (2 * SPAN * DILATIONS[-1]) == 0 and t % MERGE_ROWS == 0
    n_pairs = PART_W // LANES
    in_specs, args = [], []
    for arr in (g0, g1, g2):
        d, ln = arr.shape[1], arr.shape[2]
        for part in range(3):
            in_specs.append(pl.BlockSpec(
                (1, d, ln, LANES),
                lambda bb, jp, part=part: (bb, 0, 0, part * n_pairs + jp)))
            args.append(arr)
    return pl.pallas_call(
        _dil_kernel,
        grid=(b, n_pairs),
        in_specs=in_specs,
        out_specs=pl.BlockSpec((1, t, LANES), lambda bb, jp: (bb, 0, jp)),
        out_shape=jax.ShapeDtypeStruct((b, t, n_pairs * LANES), BF16),
        scratch_shapes=[pltpu.VMEM((3, t, LANES), F32), pltpu.VMEM((3, t, LANES), F32)],
        compiler_params=_params("parallel", "parallel"),
        name="dilated_attn",
    )(*args)


def _sb_kernel(q_ref, k_ref, v_ref, o_ref):
    tile = SB_TILE
    lane = lax.broadcasted_iota(jnp.int32, (1, LANES), 1)
    head_a = lane < HEAD_DIM
    head_masks = (head_a.astype(BF16), (lane >= HEAD_DIM).astype(BF16))
    row = lax.broadcasted_iota(jnp.int32, (tile, tile), 0)
    col = lax.broadcasted_iota(jnp.int32, (tile, tile), 1)
    tri = (row >= col).astype(BF16)
    tri2 = jnp.concatenate([tri, tri], axis=0)
    past = col < row

    def scores(qh, j, keep):
        k = k_ref[0, pl.ds(pl.multiple_of(j * tile, tile), tile), :]
        z = _nt_dot(qh, k)
        sp = jnp.maximum(z, 0.0) + jnp.log(1.0 + jnp.exp(-jnp.abs(z)))
        if keep is not None:
            sp = jnp.where(keep, sp, 0.0)
        return z, sp

    def weighted(z, sp, j, run, keep):
        v = v_ref[0, pl.ds(pl.multiple_of(j * tile, tile), tile), :]
        hi = sp.astype(BF16)
        lo = (sp - hi.astype(F32)).astype(BF16)
        csum = jnp.dot(jnp.concatenate([hi, lo], axis=1), tri2, preferred_element_type=F32)
        w = jnp.exp(z - csum - run)
        if keep is not None:
            w = jnp.where(keep, w, 0.0)
        return jnp.dot(w.astype(BF16), v, preferred_element_type=F32)

    def live(runs):
        low = jnp.minimum(jnp.min(runs[0]), jnp.min(runs[1]))
        return (low < SB_EXIT).astype(jnp.int32)

    def first_two(i, q_heads):
        has_prev = i > 0
        j_prev = jnp.maximum(i - 1, 0)
        accs, runs = [], []
        for qh in q_heads:
            z0, sp0 = scores(qh, i, past)
            z1, sp1 = scores(qh, j_prev, has_prev)
            run0 = jnp.sum(sp0, axis=1, keepdims=True)
            accs.append(weighted(z0, sp0, i, 0.0, past)
                        + weighted(z1, sp1, j_prev, run0, has_prev))
            runs.append(run0 + jnp.sum(sp1, axis=1, keepdims=True))
        return accs, runs

    def remaining(i, q_heads, accs, runs):
        def step(c):
            j = i - 2 - c[0]
            new_acc, new_run = [], []
            for h, qh in enumerate(q_heads):
                z, sp = scores(qh, j, None)
                new_acc.append(c[2 + h] + weighted(z, sp, j, c[4 + h], None))
                new_run.append(c[4 + h] + jnp.sum(sp, axis=1, keepdims=True))
            return (c[0] + 1, live(new_run), *new_acc, *new_run)

        out = lax.while_loop(lambda c: (c[0] < i - 1) & (c[1] > 0), step,
                             (jnp.int32(0), live(runs), *accs, *runs))
        return jnp.where(head_a, out[2], out[3])

    tiles = []
    for sub in range(SB_QTILES):
        i = pl.program_id(2) * SB_QTILES + sub
        q = q_ref[0, sub * tile:(sub + 1) * tile, :]
        q_heads = tuple(q * m for m in head_masks)
        tiles.append((i, q_heads) + first_two(i, q_heads))
    for sub, (i, q_heads, accs, runs) in enumerate(tiles):
        o_ref[0, sub * tile:(sub + 1) * tile, :] = remaining(i, q_heads, accs, runs).astype(BF16)


def _sb_attention(sb):
    b, t, _ = sb.shape
    n_pairs = N_SB_HEADS * HEAD_DIM // LANES
    rows = SB_TILE * SB_QTILES
    assert t % rows == 0
    kv = lambda part: pl.BlockSpec((1, t, LANES),
                                   lambda bb, jp, i: (bb, 0, part * n_pairs + jp))
    q_tile = pl.BlockSpec((1, rows, LANES), lambda bb, jp, i: (bb, i, jp))
    return pl.pallas_call(
        _sb_kernel,
        grid=(b, n_pairs, t // rows),
        in_specs=[q_tile, kv(1), kv(2)],
        out_specs=q_tile,
        out_shape=jax.ShapeDtypeStruct((b, t, n_pairs * LANES), BF16),
        compiler_params=_params("parallel", "parallel", "arbitrary"),
        name="stick_breaking_attn",
    )(sb, sb, sb)


def _out_kernel(x_ref, gain_ref, od_ref, os_ref, win_ref, wpd_ref, wps_ref, wo_ref, o_ref,
                *, gate_col0):
    x = x_ref[...]
    dm = x.shape[1]
    h = _rms(x, gain_ref[...]).astype(BF16)
    od, osb = od_ref[...], os_ref[...]
    dot = functools.partial(jnp.dot, preferred_element_type=F32)
    acc = jnp.zeros(x.shape, F32)
    for c in range(dm // OUT_CHUNK):
        cols = slice(c * OUT_CHUNK, (c + 1) * OUT_CHUNK)
        gd = slice(gate_col0 + c * OUT_CHUNK, gate_col0 + (c + 1) * OUT_CHUNK)
        gs = slice(gate_col0 + dm + c * OUT_CHUNK, gate_col0 + dm + (c + 1) * OUT_CHUNK)
        y = (_sigmoid(dot(h, win_ref[:, gd])) * dot(od, wpd_ref[:, cols])
             + _sigmoid(dot(h, win_ref[:, gs])) * dot(osb, wps_ref[:, cols]))
        acc = acc + dot(y.astype(BF16), wo_ref[cols, :])
    o_ref[...] = x + acc


def _mix_out(x2, gains, o_dil, o_sb, w_in, wpd, wps, wo, l):
    n, d = x2.shape
    assert n % ROW_TILE == 0 and d % OUT_CHUNK == 0
    row = lambda w: pl.BlockSpec((ROW_TILE, w), lambda i: (i, 0))
    return pl.pallas_call(
        functools.partial(_out_kernel, gate_col0=w_in.shape[2] - 2 * d),
        grid=(n // ROW_TILE,),
        in_specs=[row(d), _layer(gains, l), row(o_dil.shape[1]), row(o_sb.shape[1]),
                  _layer(w_in, l), _layer(wpd, l), _layer(wps, l), _layer(wo, l)],
        out_specs=row(d),
        out_shape=jax.ShapeDtypeStruct((n, d), F32),
        compiler_params=_params("parallel"),
        name="mix_out",
    )(x2, gains, o_dil, o_sb, w_in, wpd, wps, wo)


def _rope_tables(t, tm):
    half = ROPE_DIM // 2
    inv_freq = ROPE_THETA ** (-jnp.arange(0, ROPE_DIM, 2, dtype=F32) / ROPE_DIM)
    lane = jnp.arange(LANES) % HEAD_DIM
    tabs = []
    for d in DILATIONS:
        pos = jnp.arange(t, dtype=jnp.int32).reshape(t // tm, tm // d, d)
        pos = pos.transpose(0, 2, 1).reshape(t).astype(F32)
        ang = pos[:, None] * inv_freq[None, :]
        cos = jnp.cos(ang)[:, lane % half]
        sin = jnp.sin(ang)[:, lane % half]
        c = jnp.where(lane < ROPE_DIM, cos, 1.0)
        s1 = jnp.where(lane < half, -sin, 0.0)
        s2 = jnp.where((lane >= half) & (lane < ROPE_DIM), sin, 0.0)
        tabs.append(jnp.stack([c, s1, s2]).astype(F32))
    return tabs


def kernel(x, norm_ffn1, ffn1_w_gate, ffn1_w_up, ffn1_w_down, norm_mix, w_in,
           w_proj_dil, w_proj_sb, w_out, norm_ffn2, ffn2_w_gate, ffn2_w_up,
           ffn2_w_down, norm_final):
    b, t, dm = x.shape
    depth = w_in.shape[0]
    assert w_in.shape[2] == 3 * D_DIL + 3 * N_SB_HEADS * HEAD_DIM + 2 * dm
    tabs = _rope_tables(t, ROW_TILE)
    gain3 = lambda g: g.reshape(-1, 1, dm)
    n1, nm, n2, fin = gain3(norm_ffn1), gain3(norm_mix), gain3(norm_ffn2), gain3(norm_final)
    f1g, f1u, f1d = _to_bf16(ffn1_w_gate), _to_bf16(ffn1_w_up), _to_bf16(ffn1_w_down)
    f2g, f2u, f2d = _to_bf16(ffn2_w_gate), _to_bf16(ffn2_w_up), _to_bf16(ffn2_w_down)
    win, wpd, wps, wo = (_to_bf16(w_in), _to_bf16(w_proj_dil), _to_bf16(w_proj_sb),
                         _to_bf16(w_out))

    x2 = x.reshape(b * t, dm)
    for l in range(depth):
        x2 = _ffn(x2, n1, f1g, f1u, f1d, fin, l, final=False)
        g0, g1, g2, sb = _qkv(x2.reshape(b, t, dm), nm, win, tabs, l)
        o_dil = _dil_attention(g0, g1, g2)
        o_sb = _sb_attention(sb)
        x2 = _mix_out(x2, nm, o_dil.reshape(b * t, -1), o_sb.reshape(b * t, -1),
                      win, wpd, wps, wo, l)
        x2 = _ffn(x2, n2, f2g, f2u, f2d, fin, l, final=(l == depth - 1))
    return x2.reshape(b, t, dm)
```

```python
import functools

import jax
import jax.numpy as jnp
from jax import lax
from jax.experimental import pallas as pl
from jax.experimental.pallas import tpu as pltpu

F32 = jnp.float32
BF16 = jnp.bfloat16

HEAD_DIM = 64
DILATIONS = (1, 4, 16)
SPAN = 128
HEADS_PER_GROUP = 4
N_SB_HEADS = 4
ROPE_DIM = HEAD_DIM // 4
ROPE_THETA = 500000.0
RMS_EPS = 1e-6
LANES = 128
PART_W = HEADS_PER_GROUP * HEAD_DIM
GROUP_W = 3 * PART_W
D_DIL = len(DILATIONS) * PART_W
QK_SCALE = HEAD_DIM ** -0.5

ROW_TILE = 1024
FFN_ROWS = 512
FF_CHUNK = 256
OUT_CHUNK = 256
SB_TILE = 256
SB_QTILES = 2
DIL_UNROLL = 8
MERGE_ROWS = 256
CAST_BLOCK_BYTES = 4 << 20
SB_EXIT = 105.0

VMEM_LIMIT = 56 * 1024 * 1024


def _rms(x, gain):
    ms = jnp.mean(x * x, axis=-1, keepdims=True)
    return x * lax.rsqrt(ms + RMS_EPS) * gain


def _sigmoid(x):
    return 1.0 / (1.0 + jnp.exp(-x))


def _nt_dot(a, b):
    return lax.dot_general(a, b, (((1,), (1,)), ((), ())), preferred_element_type=F32)


def _layer(arr, l):
    return pl.BlockSpec((None,) + arr.shape[1:], lambda *_: (l, 0, 0),
                        pipeline_mode=pl.Buffered(1))


def _params(*semantics):
    return pltpu.CompilerParams(dimension_semantics=semantics, vmem_limit_bytes=VMEM_LIMIT)


def _cast_kernel(w_ref, o_ref):
    o_ref[...] = w_ref[...].astype(BF16)


def _to_bf16(w):
    depth, r, c = w.shape
    rows = depth * r
    tr = 1 << (min(rows, CAST_BLOCK_BYTES // (4 * c)).bit_length() - 1)
    while rows % tr:
        tr //= 2
    assert tr >= 16
    spec = pl.BlockSpec((tr, c), lambda i: (i, 0))
    out = pl.pallas_call(
        _cast_kernel, grid=(rows // tr,), in_specs=[spec], out_specs=spec,
        out_shape=jax.ShapeDtypeStruct((rows, c), BF16),
        compiler_params=_params("parallel"), name="cast_bf16",
    )(w.reshape(rows, c))
    return out.reshape(depth, r, c)


def _ffn_kernel(x_ref, gain_ref, wg_ref, wu_ref, wd_ref, fin_ref, o_ref, *, final):
    x = x_ref[...]
    h = _rms(x, gain_ref[...]).astype(BF16)
    acc = jnp.zeros(x.shape, F32)
    for c in range(wg_ref.shape[1] // FF_CHUNK):
        cols = slice(c * FF_CHUNK, (c + 1) * FF_CHUNK)
        g = jnp.dot(h, wg_ref[:, cols].astype(BF16), preferred_element_type=F32)
        u = jnp.dot(h, wu_ref[:, cols].astype(BF16), preferred_element_type=F32)
        a = (g * _sigmoid(g) * u).astype(BF16)
        acc = acc + jnp.dot(a, wd_ref[cols, :].astype(BF16), preferred_element_type=F32)
    y = x + 0.5 * acc
    if final:
        y = _rms(y, fin_ref[...])
    o_ref[...] = y


def _ffn(x2, gains, wg, wu, wd, fin_gain, l, *, final):
    n, d = x2.shape
    assert n % FFN_ROWS == 0 and wg.shape[2] % FF_CHUNK == 0
    row = pl.BlockSpec((FFN_ROWS, d), lambda i: (i, 0))
    return pl.pallas_call(
        functools.partial(_ffn_kernel, final=final),
        grid=(n // FFN_ROWS,),
        in_specs=[row, _layer(gains, l), _layer(wg, l), _layer(wu, l), _layer(wd, l),
                  _layer(fin_gain, 0)],
        out_specs=row,
        out_shape=jax.ShapeDtypeStruct((n, d), F32),
        compiler_params=_params("parallel"),
        name="ffn_final" if final else "ffn",
    )(x2, gains, wg, wu, wd, fin_gain)


def _rope(xs, tab_ref):
    return (xs * tab_ref[0]
            + pltpu.roll(xs, LANES - ROPE_DIM // 2, 1) * tab_ref[1]
            + pltpu.roll(xs, ROPE_DIM // 2, 1) * tab_ref[2])


def _project(hb, w_ref, col0, part_stride, tab_ref, out_ref, d):
    rm = hb.shape[0] // (d or 1)
    for part in range(3):
        c0 = col0 + part * part_stride
        p = jnp.dot(hb, w_ref[:, c0:c0 + PART_W], preferred_element_type=F32)
        for half in range(PART_W // LANES):
            xs = p[:, half * LANES:(half + 1) * LANES]
            if part < 2 and tab_ref is not None:
                xs = _rope(xs, tab_ref)
            if part == 0:
                xs = xs * QK_SCALE
            xb = xs.astype(BF16)
            cols = slice(part * PART_W + half * LANES, part * PART_W + (half + 1) * LANES)
            if d is None:
                out_ref[0, :, cols] = xb
            else:
                for r in range(d):
                    out_ref[0, r, :, cols] = xb[r * rm:(r + 1) * rm, :]


def _qkv_kernel(x_ref, gain_ref, w_ref, tab0_ref, tab1_ref, tab2_ref,
                g0_ref, g1_ref, g2_ref, sb_ref, h_scr, hp1_scr, hp2_scr):
    tm = x_ref.shape[1]
    h = _rms(x_ref[0], gain_ref[...])
    n_slab = h.shape[1] // LANES
    for c in range(n_slab):
        h_scr[c] = h[:, c * LANES:(c + 1) * LANES]
    hn = h.astype(BF16)
    _project(hn, w_ref, 0, D_DIL, tab0_ref, g0_ref, 1)
    _project(hn, w_ref, 3 * D_DIL, PART_W, None, sb_ref, None)
    for g, tab_ref, out_ref, hp_scr in ((1, tab1_ref, g1_ref, hp1_scr),
                                        (2, tab2_ref, g2_ref, hp2_scr)):
        d = DILATIONS[g]
        rm = tm // d
        for r in range(d):
            for c in range(n_slab):
                hp_scr[r * rm:(r + 1) * rm, c * LANES:(c + 1) * LANES] = (
                    h_scr[c, pl.ds(r, rm, stride=d), :].astype(BF16))
        _project(hp_scr[...], w_ref, g * PART_W, D_DIL, tab_ref, out_ref, d)


def _qkv(x, gains, w_in, tabs, l):
    b, t, dm = x.shape
    tm = ROW_TILE
    d1, d2 = DILATIONS[1], DILATIONS[2]
    assert t % tm == 0 and tm % (16 * d2) == 0
    tab_spec = pl.BlockSpec((3, tm, LANES), lambda i, bb: (0, i, 0))
    return pl.pallas_call(
        _qkv_kernel,
        grid=(t // tm, b),
        in_specs=[pl.BlockSpec((1, tm, dm), lambda i, bb: (bb, i, 0)),
                  _layer(gains, l), _layer(w_in, l), tab_spec, tab_spec, tab_spec],
        out_specs=[pl.BlockSpec((1, 1, tm, GROUP_W), lambda i, bb: (bb, 0, i, 0)),
                   pl.BlockSpec((1, d1, tm // d1, GROUP_W), lambda i, bb: (bb, 0, i, 0)),
                   pl.BlockSpec((1, d2, tm // d2, GROUP_W), lambda i, bb: (bb, 0, i, 0)),
                   pl.BlockSpec((1, tm, GROUP_W), lambda i, bb: (bb, i, 0))],
        out_shape=[jax.ShapeDtypeStruct((b, 1, t, GROUP_W), BF16),
                   jax.ShapeDtypeStruct((b, d1, t // d1, GROUP_W), BF16),
                   jax.ShapeDtypeStruct((b, d2, t // d2, GROUP_W), BF16),
                   jax.ShapeDtypeStruct((b, t, GROUP_W), BF16)],
        scratch_shapes=[pltpu.VMEM((dm // LANES, tm, LANES), F32),
                        pltpu.VMEM((tm, dm), BF16), pltpu.VMEM((tm, dm), BF16)],
        compiler_params=_params("parallel", "parallel"),
        name="qkv_proj",
    )(x, gains, w_in, *tabs)


def _dil_kernel(q0, k0, v0, q1, k1, v1, q2, k2, v2, o_ref, o_scr, l_scr):
    t = o_ref.shape[1]
    lane = lax.broadcasted_iota(jnp.int32, (1, LANES), 1)
    mask_a = (lane < HEAD_DIM).astype(BF16)
    mask_b = (lane >= HEAD_DIM).astype(BF16)
    head_a = lane < HEAD_DIM
    dist = (lax.broadcasted_iota(jnp.int32, (SPAN, 2 * SPAN), 0)
            - lax.broadcasted_iota(jnp.int32, (SPAN, 2 * SPAN), 1))
    bias_first = jnp.where(dist >= 0, 0.0, -1e30)
    bias_band = jnp.where((dist + SPAN >= 0) & (dist <= 0), 0.0, -1e30)

    for g, (q_ref, k_ref, v_ref) in enumerate(((q0, k0, v0), (q1, k1, v1), (q2, k2, v2))):
        d = DILATIONS[g]
        nb = t // d // SPAN

        def block(idx, carry, g=g, d=d, nb=nb, q_ref=q_ref, k_ref=k_ref, v_ref=v_ref):
            r = idx // nb
            n = idx % nb
            ws = jnp.maximum(n - 1, 0) * SPAN
            q = q_ref[0, r, pl.ds(pl.multiple_of(n * SPAN, SPAN), SPAN), :]
            kw = k_ref[0, r, pl.ds(pl.multiple_of(ws, SPAN), 2 * SPAN), :]
            vw = v_ref[0, r, pl.ds(pl.multiple_of(ws, SPAN), 2 * SPAN), :]
            bias = jnp.where(n == 0, bias_first, bias_band)
            outs, lses = [], []
            vw1 = jnp.concatenate([vw, jnp.ones_like(vw)], axis=1)
            for m_h in (mask_a, mask_b):
                s = _nt_dot(q * m_h, kw) + bias
                m = jnp.max(s, axis=1, keepdims=True)
                p = jnp.exp(s - m)
                od = jnp.dot(p.astype(BF16), vw1, preferred_element_type=F32)
                den = od[:, LANES:]
                outs.append(od[:, :LANES] / den)
                lses.append(m + jnp.log(den))
            o = jnp.where(head_a, outs[0], outs[1])
            lse = jnp.where(head_a, lses[0], lses[1])
            start = r + d * n * SPAN
            rows = pl.ds(start, SPAN) if d == 1 else pl.ds(start, SPAN, stride=d)
            o_scr[g, rows, :] = o
            l_scr[g, rows, :] = lse
            return carry

        lax.fori_loop(0, d * nb, block, 0, unroll=DIL_UNROLL)

    def merge(c, carry):
        rows = pl.ds(pl.multiple_of(c * MERGE_ROWS, MERGE_ROWS), MERGE_ROWS)
        l0, l1, l2 = l_scr[0, rows, :], l_scr[1, rows, :], l_scr[2, rows, :]
        m = jnp.maximum(jnp.maximum(l0, l1), l2)
        e0, e1, e2 = jnp.exp(l0 - m), jnp.exp(l1 - m), jnp.exp(l2 - m)
        num = e0 * o_scr[0, rows, :] + e1 * o_scr[1, rows, :] + e2 * o_scr[2, rows, :]
        o_ref[0, rows, :] = (num / (e0 + e1 + e2)).astype(BF16)
        return carry

    lax.fori_loop(0, t // MERGE_ROWS, merge, 0)


def _dil_attention(g0, g1, g2):
    b, _, t, _ = g0.shape
    assert t % (2 * SPAN * DILATIONS[-1]) == 0 and t % MERGE_ROWS == 0
    n_pairs = PART_W // LANES
    in_specs, args = [], []
    for arr in (g0, g1, g2):
        d, ln = arr.shape[1], arr.shape[2]
        for part in range(3):
            in_specs.append(pl.BlockSpec(
                (1, d, ln, LANES),
                lambda bb, jp, part=part: (bb, 0, 0, part * n_pairs + jp)))
            args.append(arr)
    return pl.pallas_call(
        _dil_kernel,
        grid=(b, n_pairs),
        in_specs=in_specs,
        out_specs=pl.BlockSpec((1, t, LANES), lambda bb, jp: (bb, 0, jp)),
        out_shape=jax.ShapeDtypeStruct((b, t, n_pairs * LANES), BF16),
        scratch_shapes=[pltpu.VMEM((3, t, LANES), F32), pltpu.VMEM((3, t, LANES), F32)],
        compiler_params=_params("parallel", "parallel"),
        name="dilated_attn",
    )(*args)


def _sb_kernel(q_ref, k_ref, v_ref, o_ref):
    tile = SB_TILE
    lane = lax.broadcasted_iota(jnp.int32, (1, LANES), 1)
    head_a = lane < HEAD_DIM
    head_masks = (head_a.astype(BF16), (lane >= HEAD_DIM).astype(BF16))
    row = lax.broadcasted_iota(jnp.int32, (tile, tile), 0)
    col = lax.broadcasted_iota(jnp.int32, (tile, tile), 1)
    tri = (row >= col).astype(BF16)
    tri2 = jnp.concatenate([tri, tri], axis=0)
    past = col < row

    def scores(qh, j, keep):
        k = k_ref[0, pl.ds(pl.multiple_of(j * tile, tile), tile), :]
        z = _nt_dot(qh, k)
        sp = jnp.maximum(z, 0.0) + jnp.log(1.0 + jnp.exp(-jnp.abs(z)))
        if keep is not None:
            sp = jnp.where(keep, sp, 0.0)
        return z, sp

    def weighted(z, sp, j, run, keep):
        v = v_ref[0, pl.ds(pl.multiple_of(j * tile, tile), tile), :]
        hi = sp.astype(BF16)
        lo = (sp - hi.astype(F32)).astype(BF16)
        csum = jnp.dot(jnp.concatenate([hi, lo], axis=1), tri2, preferred_element_type=F32)
        w = jnp.exp(z - csum - run)
        if keep is not None:
            w = jnp.where(keep, w, 0.0)
        return jnp.dot(w.astype(BF16), v, preferred_element_type=F32)

    def live(runs):
        low = jnp.minimum(jnp.min(runs[0]), jnp.min(runs[1]))
        return (low < SB_EXIT).astype(jnp.int32)

    def first_two(i, q_heads):
        has_prev = i > 0
        j_prev = jnp.maximum(i - 1, 0)
        accs, runs = [], []
        for qh in q_heads:
            z0, sp0 = scores(qh, i, past)
            z1, sp1 = scores(qh, j_prev, has_prev)
            run0 = jnp.sum(sp0, axis=1, keepdims=True)
            accs.append(weighted(z0, sp0, i, 0.0, past)
                        + weighted(z1, sp1, j_prev, run0, has_prev))
            runs.append(run0 + jnp.sum(sp1, axis=1, keepdims=True))
        return accs, runs

    def remaining(i, q_heads, accs, runs):
        def step(c):
            j = i - 2 - c[0]
            new_acc, new_run = [], []
            for h, qh in enumerate(q_heads):
                z, sp = scores(qh, j, None)
                new_acc.append(c[2 + h] + weighted(z, sp, j, c[4 + h], None))
                new_run.append(c[4 + h] + jnp.sum(sp, axis=1, keepdims=True))
            return (c[0] + 1, live(new_run), *new_acc, *new_run)

        out = lax.while_loop(lambda c: (c[0] < i - 1) & (c[1] > 0), step,
                             (jnp.int32(0), live(runs), *accs, *runs))
        return jnp.where(head_a, out[2], out[3])

    tiles = []
    for sub in range(SB_QTILES):
        i = pl.program_id(2) * SB_QTILES + sub
        q = q_ref[0, sub * tile:(sub + 1) * tile, :]
        q_heads = tuple(q * m for m in head_masks)
        tiles.append((i, q_heads) + first_two(i, q_heads))
    for sub, (i, q_heads, accs, runs) in enumerate(tiles):
        o_ref[0, sub * tile:(sub + 1) * tile, :] = remaining(i, q_heads, accs, runs).astype(BF16)


def _sb_attention(sb):
    b, t, _ = sb.shape
    n_pairs = N_SB_HEADS * HEAD_DIM // LANES
    rows = SB_TILE * SB_QTILES
    assert t % rows == 0
    kv = lambda part: pl.BlockSpec((1, t, LANES),
                                   lambda bb, jp, i: (bb, 0, part * n_pairs + jp))
    q_tile = pl.BlockSpec((1, rows, LANES), lambda bb, jp, i: (bb, i, jp))
    return pl.pallas_call(
        _sb_kernel,
        grid=(b, n_pairs, t // rows),
        in_specs=[q_tile, kv(1), kv(2)],
        out_specs=q_tile,
        out_shape=jax.ShapeDtypeStruct((b, t, n_pairs * LANES), BF16),
        compiler_params=_params("parallel", "parallel", "arbitrary"),
        name="stick_breaking_attn",
    )(sb, sb, sb)


def _out_kernel(x_ref, gain_ref, od_ref, os_ref, win_ref, wpd_ref, wps_ref, wo_ref, o_ref,
                *, gate_col0):
    x = x_ref[...]
    dm = x.shape[1]
    h = _rms(x, gain_ref[...]).astype(BF16)
    od, osb = od_ref[...], os_ref[...]
    dot = functools.partial(jnp.dot, preferred_element_type=F32)
    acc = jnp.zeros(x.shape, F32)
    for c in range(dm // OUT_CHUNK):
        cols = slice(c * OUT_CHUNK, (c + 1) * OUT_CHUNK)
        gd = slice(gate_col0 + c * OUT_CHUNK, gate_col0 + (c + 1) * OUT_CHUNK)
        gs = slice(gate_col0 + dm + c * OUT_CHUNK, gate_col0 + dm + (c + 1) * OUT_CHUNK)
        y = (_sigmoid(dot(h, win_ref[:, gd])) * dot(od, wpd_ref[:, cols])
             + _sigmoid(dot(h, win_ref[:, gs])) * dot(osb, wps_ref[:, cols]))
        acc = acc + dot(y.astype(BF16), wo_ref[cols, :])
    o_ref[...] = x + acc


def _mix_out(x2, gains, o_dil, o_sb, w_in, wpd, wps, wo, l):
    n, d = x2.shape
    assert n % ROW_TILE == 0 and d % OUT_CHUNK == 0
    row = lambda w: pl.BlockSpec((ROW_TILE, w), lambda i: (i, 0))
    return pl.pallas_call(
        functools.partial(_out_kernel, gate_col0=w_in.shape[2] - 2 * d),
        grid=(n // ROW_TILE,),
        in_specs=[row(d), _layer(gains, l), row(o_dil.shape[1]), row(o_sb.shape[1]),
                  _layer(w_in, l), _layer(wpd, l), _layer(wps, l), _layer(wo, l)],
        out_specs=row(d),
        out_shape=jax.ShapeDtypeStruct((n, d), F32),
        compiler_params=_params("parallel"),
        name="mix_out",
    )(x2, gains, o_dil, o_sb, w_in, wpd, wps, wo)


def _rope_tables(t, tm):
    half = ROPE_DIM // 2
    inv_freq = ROPE_THETA ** (-jnp.arange(0, ROPE_DIM, 2, dtype=F32) / ROPE_DIM)
    lane = jnp.arange(LANES) % HEAD_DIM
    tabs = []
    for d in DILATIONS:
        pos = jnp.arange(t, dtype=jnp.int32).reshape(t // tm, tm // d, d)
        pos = pos.transpose(0, 2, 1).reshape(t).astype(F32)
        ang = pos[:, None] * inv_freq[None, :]
        cos = jnp.cos(ang)[:, lane % half]
        sin = jnp.sin(ang)[:, lane % half]
        c = jnp.where(lane < ROPE_DIM, cos, 1.0)
        s1 = jnp.where(lane < half, -sin, 0.0)
        s2 = jnp.where((lane >= half) & (lane < ROPE_DIM), sin, 0.0)
        tabs.append(jnp.stack([c, s1, s2]).astype(F32))
    return tabs


def kernel(x, norm_ffn1, ffn1_w_gate, ffn1_w_up, ffn1_w_down, norm_mix, w_in,
           w_proj_dil, w_proj_sb, w_out, norm_ffn2, ffn2_w_gate, ffn2_w_up,
           ffn2_w_down, norm_final):
    b, t, dm = x.shape
    depth = w_in.shape[0]
    assert w_in.shape[2] == 3 * D_DIL + 3 * N_SB_HEADS * HEAD_DIM + 2 * dm
    tabs = _rope_tables(t, ROW_TILE)
    gain3 = lambda g: g.reshape(-1, 1, dm)
    n1, nm, n2, fin = gain3(norm_ffn1), gain3(norm_mix), gain3(norm_ffn2), gain3(norm_final)
    f1g, f1u, f1d = ffn1_w_gate, ffn1_w_up, ffn1_w_down
    f2g, f2u, f2d = ffn2_w_gate, ffn2_w_up, ffn2_w_down
    win, wpd, wps, wo = (_to_bf16(w_in), _to_bf16(w_proj_dil), _to_bf16(w_proj_sb),
                         _to_bf16(w_out))

    x2 = x.reshape(b * t, dm)
    for l in range(depth):
        x2 = _ffn(x2, n1, f1g, f1u, f1d, fin, l, final=False)
        g0, g1, g2, sb = _qkv(x2.reshape(b, t, dm), nm, win, tabs, l)
        o_dil = _dil_attention(g0, g1, g2)
        o_sb = _sb_attention(sb)
        x2 = _mix_out(x2, nm, o_dil.reshape(b * t, -1), o_sb.reshape(b * t, -1),
                      win, wpd, wps, wo, l)
        x2 = _ffn(x2, n2, f2g, f2u, f2d, fin, l, final=(l == depth - 1))
    return x2.reshape(b, t, dm)
```

```python
import functools

import jax
import jax.numpy as jnp
from jax import lax
from jax.experimental import pallas as pl
from jax.experimental.pallas import tpu as pltpu

F32 = jnp.float32
BF16 = jnp.bfloat16

HEAD_DIM = 64
DILATIONS = (1, 4, 16)
SPAN = 128
HEADS_PER_GROUP = 4
N_SB_HEADS = 4
ROPE_DIM = HEAD_DIM // 4
ROPE_THETA = 500000.0
RMS_EPS = 1e-6
LANES = 128
PART_W = HEADS_PER_GROUP * HEAD_DIM
GROUP_W = 3 * PART_W
D_DIL = len(DILATIONS) * PART_W
QK_SCALE = HEAD_DIM ** -0.5

ROW_TILE = 1024
FFN_ROWS = 512
FF_CHUNK = 256
W_IN_BLOCK = 1024
OUT_CHUNK = 256
SB_TILE = 256
SB_QTILES = 2
DIL_UNROLL = 8
MERGE_ROWS = 256
SB_EXIT = 105.0

VMEM_LIMIT = 56 * 1024 * 1024


def _rms(x, gain):
    ms = jnp.mean(x * x, axis=-1, keepdims=True)
    return x * lax.rsqrt(ms + RMS_EPS) * gain


def _sigmoid(x):
    return 1.0 / (1.0 + jnp.exp(-x))


def _nt_dot(a, b):
    return lax.dot_general(a, b, (((1,), (1,)), ((), ())), preferred_element_type=F32)


def _w(ref, rows, cols):
    return ref[rows, cols].astype(BF16)


ALL = slice(None)


def _layer(arr, l):
    return pl.BlockSpec((None,) + arr.shape[1:], lambda *_: (l, 0, 0),
                        pipeline_mode=pl.Buffered(1))


def _layer_cols(arr, l, cb):
    return pl.BlockSpec((None, arr.shape[1], W_IN_BLOCK), lambda *_: (l, 0, cb),
                        pipeline_mode=pl.Buffered(1))


def _params(*semantics):
    return pltpu.CompilerParams(dimension_semantics=semantics, vmem_limit_bytes=VMEM_LIMIT)


def _ffn_kernel(x_ref, gain_ref, wg_ref, wu_ref, wd_ref, fin_ref, o_ref, *, final):
    x = x_ref[...]
    h = _rms(x, gain_ref[...]).astype(BF16)
    acc = jnp.zeros(x.shape, F32)
    for c in range(wg_ref.shape[1] // FF_CHUNK):
        cols = slice(c * FF_CHUNK, (c + 1) * FF_CHUNK)
        g = jnp.dot(h, _w(wg_ref, ALL, cols), preferred_element_type=F32)
        u = jnp.dot(h, _w(wu_ref, ALL, cols), preferred_element_type=F32)
        a = (g * _sigmoid(g) * u).astype(BF16)
        acc = acc + jnp.dot(a, _w(wd_ref, cols, ALL), preferred_element_type=F32)
    y = x + 0.5 * acc
    if final:
        y = _rms(y, fin_ref[...])
    o_ref[...] = y


def _ffn(x2, gains, wg, wu, wd, fin_gain, l, *, final):
    n, d = x2.shape
    assert n % FFN_ROWS == 0 and wg.shape[2] % FF_CHUNK == 0
    row = pl.BlockSpec((FFN_ROWS, d), lambda i: (i, 0))
    return pl.pallas_call(
        functools.partial(_ffn_kernel, final=final),
        grid=(n // FFN_ROWS,),
        in_specs=[row, _layer(gains, l), _layer(wg, l), _layer(wu, l), _layer(wd, l),
                  _layer(fin_gain, 0)],
        out_specs=row,
        out_shape=jax.ShapeDtypeStruct((n, d), F32),
        compiler_params=_params("parallel"),
        name="ffn_final" if final else "ffn",
    )(x2, gains, wg, wu, wd, fin_gain)


def _rope(xs, tab_ref):
    return (xs * tab_ref[0]
            + pltpu.roll(xs, LANES - ROPE_DIM // 2, 1) * tab_ref[1]
            + pltpu.roll(xs, ROPE_DIM // 2, 1) * tab_ref[2])


def _project(hb, w_refs, col0, part_stride, tab_ref, out_ref, d, stage):
    rm = hb.shape[0] // (d or 1)
    blk_w = w_refs[0].shape[1]
    for part in range(3):
        c0 = col0 + part * part_stride
        w_ref, off = w_refs[c0 // blk_w], c0 % blk_w
        p = jnp.dot(hb, _w(w_ref, ALL, slice(off, off + PART_W)), preferred_element_type=F32)
        for half in range(PART_W // LANES):
            xs = p[:, half * LANES:(half + 1) * LANES]
            if part < 2 and tab_ref is not None:
                xs = _rope(xs, tab_ref)
            if part == 0:
                xs = xs * QK_SCALE
            cols = slice(part * PART_W + half * LANES, part * PART_W + (half + 1) * LANES)
            if d is None:
                out_ref[0, :, cols] = xs.astype(BF16)
            elif d == 1:
                out_ref[0, 0, :, cols] = xs.astype(BF16)
            else:
                slot = stage.pop()
                slot[...] = xs
                for r in range(d):
                    out_ref[0, r, :, cols] = slot[pl.ds(r, rm, stride=d), :].astype(BF16)


def _qkv_kernel(x_ref, gain_ref, wa_ref, wb_ref, wc_ref, tab_ref,
                g0_ref, g1_ref, g2_ref, sb_ref, stage_scr):
    hn = _rms(x_ref[0], gain_ref[...]).astype(BF16)
    w_refs = (wa_ref, wb_ref, wc_ref)
    stage = [stage_scr.at[k] for k in range(stage_scr.shape[0])]
    _project(hn, w_refs, 2 * PART_W, D_DIL, tab_ref, g2_ref, DILATIONS[2], stage)
    _project(hn, w_refs, PART_W, D_DIL, tab_ref, g1_ref, DILATIONS[1], stage)
    _project(hn, w_refs, 0, D_DIL, tab_ref, g0_ref, 1, stage)
    _project(hn, w_refs, 3 * D_DIL, PART_W, None, sb_ref, None, stage)


def _qkv(x, gains, w_in, tab, l):
    b, t, dm = x.shape
    tm = ROW_TILE
    d1, d2 = DILATIONS[1], DILATIONS[2]
    qkv_w = 3 * D_DIL + GROUP_W
    assert t % tm == 0 and tm % (8 * d2) == 0 and qkv_w % W_IN_BLOCK == 0
    n_stage = 2 * GROUP_W // LANES
    return pl.pallas_call(
        _qkv_kernel,
        grid=(t // tm, b),
        in_specs=[pl.BlockSpec((1, tm, dm), lambda i, bb: (bb, i, 0)), _layer(gains, l)]
                 + [_layer_cols(w_in, l, cb) for cb in range(qkv_w // W_IN_BLOCK)]
                 + [pl.BlockSpec((3, tm, LANES), lambda i, bb: (0, i, 0))],
        out_specs=[pl.BlockSpec((1, 1, tm, GROUP_W), lambda i, bb: (bb, 0, i, 0)),
                   pl.BlockSpec((1, d1, tm // d1, GROUP_W), lambda i, bb: (bb, 0, i, 0)),
                   pl.BlockSpec((1, d2, tm // d2, GROUP_W), lambda i, bb: (bb, 0, i, 0)),
                   pl.BlockSpec((1, tm, GROUP_W), lambda i, bb: (bb, i, 0))],
        out_shape=[jax.ShapeDtypeStruct((b, 1, t, GROUP_W), BF16),
                   jax.ShapeDtypeStruct((b, d1, t // d1, GROUP_W), BF16),
                   jax.ShapeDtypeStruct((b, d2, t // d2, GROUP_W), BF16),
                   jax.ShapeDtypeStruct((b, t, GROUP_W), BF16)],
        scratch_shapes=[pltpu.VMEM((n_stage, tm, LANES), F32)],
        compiler_params=_params("parallel", "parallel"),
        name="qkv_proj",
    )(x, gains, w_in, w_in, w_in, tab)


def _dil_kernel(q0, k0, v0, q1, k1, v1, q2, k2, v2, o_ref, o_scr, l_scr):
    t = o_ref.shape[1]
    lane = lax.broadcasted_iota(jnp.int32, (1, LANES), 1)
    mask_a = (lane < HEAD_DIM).astype(BF16)
    mask_b = (lane >= HEAD_DIM).astype(BF16)
    head_a = lane < HEAD_DIM
    dist = (lax.broadcasted_iota(jnp.int32, (SPAN, 2 * SPAN), 0)
            - lax.broadcasted_iota(jnp.int32, (SPAN, 2 * SPAN), 1))
    bias_first = jnp.where(dist >= 0, 0.0, -1e30)
    bias_band = jnp.where((dist + SPAN >= 0) & (dist <= 0), 0.0, -1e30)

    for g, (q_ref, k_ref, v_ref) in enumerate(((q0, k0, v0), (q1, k1, v1), (q2, k2, v2))):
        d = DILATIONS[g]
        nb = t // d // SPAN

        def block(idx, carry, g=g, d=d, nb=nb, q_ref=q_ref, k_ref=k_ref, v_ref=v_ref):
            r = idx // nb
            n = idx % nb
            ws = jnp.maximum(n - 1, 0) * SPAN
            q = q_ref[0, r, pl.ds(pl.multiple_of(n * SPAN, SPAN), SPAN), :]
            kw = k_ref[0, r, pl.ds(pl.multiple_of(ws, SPAN), 2 * SPAN), :]
            vw = v_ref[0, r, pl.ds(pl.multiple_of(ws, SPAN), 2 * SPAN), :]
            bias = jnp.where(n == 0, bias_first, bias_band)
            outs, lses = [], []
            vw1 = jnp.concatenate([vw, jnp.ones_like(vw)], axis=1)
            for m_h in (mask_a, mask_b):
                s = _nt_dot(q * m_h, kw) + bias
                m = jnp.max(s, axis=1, keepdims=True)
                p = jnp.exp(s - m)
                od = jnp.dot(p.astype(BF16), vw1, preferred_element_type=F32)
                den = od[:, LANES:]
                outs.append(od[:, :LANES] / den)
                lses.append(m + jnp.log(den))
            o = jnp.where(head_a, outs[0], outs[1])
            lse = jnp.where(head_a, lses[0], lses[1])
            start = r + d * n * SPAN
            rows = pl.ds(start, SPAN) if d == 1 else pl.ds(start, SPAN, stride=d)
            o_scr[g, rows, :] = o
            l_scr[g, rows, :] = lse
            return carry

        lax.fori_loop(0, d * nb, block, 0, unroll=DIL_UNROLL)

    def merge(c, carry):
        rows = pl.ds(pl.multiple_of(c * MERGE_ROWS, MERGE_ROWS), MERGE_ROWS)
        l0, l1, l2 = l_scr[0, rows, :], l_scr[1, rows, :], l_scr[2, rows, :]
        m = jnp.maximum(jnp.maximum(l0, l1), l2)
        e0, e1, e2 = jnp.exp(l0 - m), jnp.exp(l1 - m), jnp.exp(l2 - m)
        num = e0 * o_scr[0, rows, :] + e1 * o_scr[1, rows, :] + e2 * o_scr[2, rows, :]
        o_ref[0, rows, :] = (num / (e0 + e1 + e2)).astype(BF16)
        return carry

    lax.fori_loop(0, t // MERGE_ROWS, merge, 0)


def _dil_attention(g0, g1, g2):
    b, _, t, _ = g0.shape
    assert t % (2 * SPAN * DILATIONS[-1]) == 0 and t % MERGE_ROWS == 0
    n_pairs = PART_W // LANES
    in_specs, args = [], []
    for arr in (g0, g1, g2):
        d, ln = arr.shape[1], arr.shape[2]
        for part in range(3):
            in_specs.append(pl.BlockSpec(
                (1, d, ln, LANES),
                lambda bb, jp, part=part: (bb, 0, 0, part * n_pairs + jp)))
            args.append(arr)
    return pl.pallas_call(
        _dil_kernel,
        grid=(b, n_pairs),
        in_specs=in_specs,
        out_specs=pl.BlockSpec((1, t, LANES), lambda bb, jp: (bb, 0, jp)),
        out_shape=jax.ShapeDtypeStruct((b, t, n_pairs * LANES), BF16),
        scratch_shapes=[pltpu.VMEM((3, t, LANES), F32), pltpu.VMEM((3, t, LANES), F32)],
        compiler_params=_params("parallel", "parallel"),
        name="dilated_attn",
    )(*args)


def _sb_kernel(q_ref, k_ref, v_ref, o_ref):
    tile = SB_TILE
    lane = lax.broadcasted_iota(jnp.int32, (1, LANES), 1)
    head_a = lane < HEAD_DIM
    head_masks = (head_a.astype(BF16), (lane >= HEAD_DIM).astype(BF16))
    row = lax.broadcasted_iota(jnp.int32, (tile, tile), 0)
    col = lax.broadcasted_iota(jnp.int32, (tile, tile), 1)
    tri = (row >= col).astype(BF16)
    tri2 = jnp.concatenate([tri, tri], axis=0)
    past = col < row

    def scores(qh, j, keep):
        k = k_ref[0, pl.ds(pl.multiple_of(j * tile, tile), tile), :]
        z = _nt_dot(qh, k)
        sp = jnp.maximum(z, 0.0) + jnp.log(1.0 + jnp.exp(-jnp.abs(z)))
        if keep is not None:
            sp = jnp.where(keep, sp, 0.0)
        return z, sp

    def weighted(z, sp, j, run, keep):
        v = v_ref[0, pl.ds(pl.multiple_of(j * tile, tile), tile), :]
        hi = sp.astype(BF16)
        lo = (sp - hi.astype(F32)).astype(BF16)
        csum = jnp.dot(jnp.concatenate([hi, lo], axis=1), tri2, preferred_element_type=F32)
        w = jnp.exp(z - csum - run)
        if keep is not None:
            w = jnp.where(keep, w, 0.0)
        return jnp.dot(w.astype(BF16), v, preferred_element_type=F32)

    def live(runs):
        low = jnp.minimum(jnp.min(runs[0]), jnp.min(runs[1]))
        return (low < SB_EXIT).astype(jnp.int32)

    def first_two(i, q_heads):
        has_prev = i > 0
        j_prev = jnp.maximum(i - 1, 0)
        accs, runs = [], []
        for qh in q_heads:
            z0, sp0 = scores(qh, i, past)
            z1, sp1 = scores(qh, j_prev, has_prev)
            run0 = jnp.sum(sp0, axis=1, keepdims=True)
            accs.append(weighted(z0, sp0, i, 0.0, past)
                        + weighted(z1, sp1, j_prev, run0, has_prev))
            runs.append(run0 + jnp.sum(sp1, axis=1, keepdims=True))
        return accs, runs

    def remaining(i, q_heads, accs, runs):
        def step(c):
            j = i - 2 - c[0]
            new_acc, new_run = [], []
            for h, qh in enumerate(q_heads):
                z, sp = scores(qh, j, None)
                new_acc.append(c[2 + h] + weighted(z, sp, j, c[4 + h], None))
                new_run.append(c[4 + h] + jnp.sum(sp, axis=1, keepdims=True))
            return (c[0] + 1, live(new_run), *new_acc, *new_run)

        out = lax.while_loop(lambda c: (c[0] < i - 1) & (c[1] > 0), step,
                             (jnp.int32(0), live(runs), *accs, *runs))
        return jnp.where(head_a, out[2], out[3])

    tiles = []
    for sub in range(SB_QTILES):
        i = pl.program_id(2) * SB_QTILES + sub
        q = q_ref[0, sub * tile:(sub + 1) * tile, :]
        q_heads = tuple(q * m for m in head_masks)
        tiles.append((i, q_heads) + first_two(i, q_heads))
    for sub, (i, q_heads, accs, runs) in enumerate(tiles):
        o_ref[0, sub * tile:(sub + 1) * tile, :] = remaining(i, q_heads, accs, runs).astype(BF16)


def _sb_attention(sb):
    b, t, _ = sb.shape
    n_pairs = N_SB_HEADS * HEAD_DIM // LANES
    rows = SB_TILE * SB_QTILES
    assert t % rows == 0
    kv = lambda part: pl.BlockSpec((1, t, LANES),
                                   lambda bb, jp, i: (bb, 0, part * n_pairs + jp))
    q_tile = pl.BlockSpec((1, rows, LANES), lambda bb, jp, i: (bb, i, jp))
    return pl.pallas_call(
        _sb_kernel,
        grid=(b, n_pairs, t // rows),
        in_specs=[q_tile, kv(1), kv(2)],
        out_specs=q_tile,
        out_shape=jax.ShapeDtypeStruct((b, t, n_pairs * LANES), BF16),
        compiler_params=_params("parallel", "parallel", "arbitrary"),
        name="stick_breaking_attn",
    )(sb, sb, sb)


def _out_kernel(x_ref, gain_ref, od_ref, os_ref, wgd_ref, wgs_ref, wpd_ref, wps_ref, wo_ref,
                o_ref):
    x = x_ref[...]
    h = _rms(x, gain_ref[...]).astype(BF16)
    od, osb = od_ref[...], os_ref[...]
    dot = functools.partial(jnp.dot, preferred_element_type=F32)
    acc = jnp.zeros(x.shape, F32)
    for c in range(x.shape[1] // OUT_CHUNK):
        cols = slice(c * OUT_CHUNK, (c + 1) * OUT_CHUNK)
        y = (_sigmoid(dot(h, _w(wgd_ref, ALL, cols))) * dot(od, _w(wpd_ref, ALL, cols))
             + _sigmoid(dot(h, _w(wgs_ref, ALL, cols))) * dot(osb, _w(wps_ref, ALL, cols)))
        acc = acc + dot(y.astype(BF16), _w(wo_ref, cols, ALL))
    o_ref[...] = x + acc


def _mix_out(x2, gains, o_dil, o_sb, w_in, wpd, wps, wo, l):
    n, d = x2.shape
    assert n % ROW_TILE == 0 and d % OUT_CHUNK == 0 and d == W_IN_BLOCK
    gate_cb = w_in.shape[2] // W_IN_BLOCK - 2
    row = lambda w: pl.BlockSpec((ROW_TILE, w), lambda i: (i, 0))
    return pl.pallas_call(
        _out_kernel,
        grid=(n // ROW_TILE,),
        in_specs=[row(d), _layer(gains, l), row(o_dil.shape[1]), row(o_sb.shape[1]),
                  _layer_cols(w_in, l, gate_cb), _layer_cols(w_in, l, gate_cb + 1),
                  _layer(wpd, l), _layer(wps, l), _layer(wo, l)],
        out_specs=row(d),
        out_shape=jax.ShapeDtypeStruct((n, d), F32),
        compiler_params=_params("parallel"),
        name="mix_out",
    )(x2, gains, o_dil, o_sb, w_in, w_in, wpd, wps, wo)


def _rope_table(t):
    half = ROPE_DIM // 2
    inv_freq = ROPE_THETA ** (-jnp.arange(0, ROPE_DIM, 2, dtype=F32) / ROPE_DIM)
    lane = jnp.arange(LANES) % HEAD_DIM
    ang = jnp.arange(t, dtype=F32)[:, None] * inv_freq[None, :]
    cos = jnp.cos(ang)[:, lane % half]
    sin = jnp.sin(ang)[:, lane % half]
    c = jnp.where(lane < ROPE_DIM, cos, 1.0)
    s1 = jnp.where(lane < half, -sin, 0.0)
    s2 = jnp.where((lane >= half) & (lane < ROPE_DIM), sin, 0.0)
    return jnp.stack([c, s1, s2]).astype(F32)


def kernel(x, norm_ffn1, ffn1_w_gate, ffn1_w_up, ffn1_w_down, norm_mix, w_in,
           w_proj_dil, w_proj_sb, w_out, norm_ffn2, ffn2_w_gate, ffn2_w_up,
           ffn2_w_down, norm_final):
    b, t, dm = x.shape
    depth = w_in.shape[0]
    assert w_in.shape[2] == 3 * D_DIL + 3 * N_SB_HEADS * HEAD_DIM + 2 * dm
    tab = _rope_table(t)
    gain3 = lambda g: g.reshape(-1, 1, dm)
    n1, nm, n2, fin = gain3(norm_ffn1), gain3(norm_mix), gain3(norm_ffn2), gain3(norm_final)

    x2 = x.reshape(b * t, dm)
    for l in range(depth):
        x2 = _ffn(x2, n1, ffn1_w_gate, ffn1_w_up, ffn1_w_down, fin, l, final=False)
        g0, g1, g2, sb = _qkv(x2.reshape(b, t, dm), nm, w_in, tab, l)
        o_dil = _dil_attention(g0, g1, g2)
        o_sb = _sb_attention(sb)
        x2 = _mix_out(x2, nm, o_dil.reshape(b * t, -1), o_sb.reshape(b * t, -1),
                      w_in, w_proj_dil, w_proj_sb, w_out, l)
        x2 = _ffn(x2, n2, ffn2_w_gate, ffn2_w_up, ffn2_w_down, fin, l,
                  final=(l == depth - 1))
    return x2.reshape(b, t, dm)
```

```python
import functools

import jax
import jax.numpy as jnp
from jax import lax
from jax.experimental import pallas as pl
from jax.experimental.pallas import tpu as pltpu

F32 = jnp.float32
BF16 = jnp.bfloat16

HEAD_DIM = 64
DILATIONS = (1, 4, 16)
SPAN = 128
HEADS_PER_GROUP = 4
N_SB_HEADS = 4
ROPE_DIM = HEAD_DIM // 4
ROPE_THETA = 500000.0
RMS_EPS = 1e-6
LANES = 128
PART_W = HEADS_PER_GROUP * HEAD_DIM
GROUP_W = 3 * PART_W
D_DIL = len(DILATIONS) * PART_W
QK_SCALE = HEAD_DIM ** -0.5

ROW_TILE = 1024
FFN_ROWS = 1024
FF_CHUNK = 256
W_IN_BLOCK = 1024
OUT_CHUNK = 256
SB_TILE = 256
SB_QTILES = 4
DIL_UNROLL = 8
MERGE_ROWS = 256
SB_EXIT = 105.0

VMEM_LIMIT = 56 * 1024 * 1024
VMEM_PHYSICAL = 64 * 1024 * 1024


def _rms(x, gain):
    ms = jnp.mean(x * x, axis=-1, keepdims=True)
    return x * lax.rsqrt(ms + RMS_EPS) * gain


def _sigmoid(x):
    return 1.0 / (1.0 + jnp.exp(-x))


def _nt_dot(a, b):
    return lax.dot_general(a, b, (((1,), (1,)), ((), ())), preferred_element_type=F32)


def _w(ref, rows, cols):
    return ref[rows, cols].astype(BF16)


ALL = slice(None)


def _layer(arr, l):
    return pl.BlockSpec((None,) + arr.shape[1:], lambda *_: (l, 0, 0),
                        pipeline_mode=pl.Buffered(1))


def _layer_cols(arr, l, cb):
    return pl.BlockSpec((None, arr.shape[1], W_IN_BLOCK), lambda *_: (l, 0, cb),
                        pipeline_mode=pl.Buffered(1))


def _params(*semantics, vmem=VMEM_LIMIT):
    assert vmem < VMEM_PHYSICAL
    return pltpu.CompilerParams(dimension_semantics=semantics, vmem_limit_bytes=vmem)


def _ffn_kernel(x_ref, gain_ref, wg_ref, wu_ref, wd_ref, fin_ref, o_ref, *, final):
    x = x_ref[...]
    h = _rms(x, gain_ref[...]).astype(BF16)
    acc = jnp.zeros(x.shape, F32)
    for c in range(wg_ref.shape[1] // FF_CHUNK):
        cols = slice(c * FF_CHUNK, (c + 1) * FF_CHUNK)
        g = jnp.dot(h, _w(wg_ref, ALL, cols), preferred_element_type=F32)
        u = jnp.dot(h, _w(wu_ref, ALL, cols), preferred_element_type=F32)
        a = (g * _sigmoid(g) * u).astype(BF16)
        acc = acc + jnp.dot(a, _w(wd_ref, cols, ALL), preferred_element_type=F32)
    y = x + 0.5 * acc
    if final:
        y = _rms(y, fin_ref[...])
    o_ref[...] = y


def _ffn(x2, gains, wg, wu, wd, fin_gain, l, *, final):
    n, d = x2.shape
    assert n % FFN_ROWS == 0 and wg.shape[2] % FF_CHUNK == 0
    row = pl.BlockSpec((FFN_ROWS, d), lambda i: (i, 0))
    vmem = 4 * (3 * d * wg.shape[2] + 5 * FFN_ROWS * d) + (4 << 20)
    return pl.pallas_call(
        functools.partial(_ffn_kernel, final=final),
        grid=(n // FFN_ROWS,),
        in_specs=[row, _layer(gains, l), _layer(wg, l), _layer(wu, l), _layer(wd, l),
                  _layer(fin_gain, 0)],
        out_specs=row,
        out_shape=jax.ShapeDtypeStruct((n, d), F32),
        compiler_params=_params("parallel", vmem=vmem),
        name="ffn_final" if final else "ffn",
    )(x2, gains, wg, wu, wd, fin_gain)


def _rope(xs, tab_ref):
    return (xs * tab_ref[0]
            + pltpu.roll(xs, LANES - ROPE_DIM // 2, 1) * tab_ref[1]
            + pltpu.roll(xs, ROPE_DIM // 2, 1) * tab_ref[2])


def _project(hb, w_refs, col0, part_stride, tab_ref, out_ref, d, stage):
    rm = hb.shape[0] // (d or 1)
    blk_w = w_refs[0].shape[1]
    for part in range(3):
        c0 = col0 + part * part_stride
        w_ref, off = w_refs[c0 // blk_w], c0 % blk_w
        p = jnp.dot(hb, _w(w_ref, ALL, slice(off, off + PART_W)), preferred_element_type=F32)
        for half in range(PART_W // LANES):
            xs = p[:, half * LANES:(half + 1) * LANES]
            if part < 2 and tab_ref is not None:
                xs = _rope(xs, tab_ref)
            if part == 0:
                xs = xs * QK_SCALE
            cols = slice(part * PART_W + half * LANES, part * PART_W + (half + 1) * LANES)
            if d is None:
                out_ref[0, :, cols] = xs.astype(BF16)
            elif d == 1:
                out_ref[0, 0, :, cols] = xs.astype(BF16)
            else:
                slot = stage.pop()
                slot[...] = xs
                for r in range(d):
                    out_ref[0, r, :, cols] = slot[pl.ds(r, rm, stride=d), :].astype(BF16)


def _qkv_kernel(x_ref, gain_ref, wa_ref, wb_ref, wc_ref, tab_ref,
                g0_ref, g1_ref, g2_ref, sb_ref, stage_scr):
    hn = _rms(x_ref[0], gain_ref[...]).astype(BF16)
    w_refs = (wa_ref, wb_ref, wc_ref)
    stage = [stage_scr.at[k] for k in range(stage_scr.shape[0])]
    _project(hn, w_refs, 2 * PART_W, D_DIL, tab_ref, g2_ref, DILATIONS[2], stage)
    _project(hn, w_refs, PART_W, D_DIL, tab_ref, g1_ref, DILATIONS[1], stage)
    _project(hn, w_refs, 0, D_DIL, tab_ref, g0_ref, 1, stage)
    _project(hn, w_refs, 3 * D_DIL, PART_W, None, sb_ref, None, stage)


def _qkv(x, gains, w_in, tab, l):
    b, t, dm = x.shape
    tm = ROW_TILE
    d1, d2 = DILATIONS[1], DILATIONS[2]
    qkv_w = 3 * D_DIL + GROUP_W
    assert t % tm == 0 and tm % (8 * d2) == 0 and qkv_w % W_IN_BLOCK == 0
    n_stage = 2 * GROUP_W // LANES
    return pl.pallas_call(
        _qkv_kernel,
        grid=(t // tm, b),
        in_specs=[pl.BlockSpec((1, tm, dm), lambda i, bb: (bb, i, 0)), _layer(gains, l)]
                 + [_layer_cols(w_in, l, cb) for cb in range(qkv_w // W_IN_BLOCK)]
                 + [pl.BlockSpec((3, tm, LANES), lambda i, bb: (0, i, 0))],
        out_specs=[pl.BlockSpec((1, 1, tm, GROUP_W), lambda i, bb: (bb, 0, i, 0)),
                   pl.BlockSpec((1, d1, tm // d1, GROUP_W), lambda i, bb: (bb, 0, i, 0)),
                   pl.BlockSpec((1, d2, tm // d2, GROUP_W), lambda i, bb: (bb, 0, i, 0)),
                   pl.BlockSpec((1, tm, GROUP_W), lambda i, bb: (bb, i, 0))],
        out_shape=[jax.ShapeDtypeStruct((b, 1, t, GROUP_W), BF16),
                   jax.ShapeDtypeStruct((b, d1, t // d1, GROUP_W), BF16),
                   jax.ShapeDtypeStruct((b, d2, t // d2, GROUP_W), BF16),
                   jax.ShapeDtypeStruct((b, t, GROUP_W), BF16)],
        scratch_shapes=[pltpu.VMEM((n_stage, tm, LANES), F32)],
        compiler_params=_params("parallel", "parallel"),
        name="qkv_proj",
    )(x, gains, w_in, w_in, w_in, tab)


def _dil_kernel(q0, k0, v0, q1, k1, v1, q2, k2, v2, o_ref, o_scr, l_scr):
    t = o_ref.shape[1]
    lane = lax.broadcasted_iota(jnp.int32, (1, LANES), 1)
    mask_a = (lane < HEAD_DIM).astype(BF16)
    mask_b = (lane >= HEAD_DIM).astype(BF16)
    head_a = lane < HEAD_DIM
    dist = (lax.broadcasted_iota(jnp.int32, (SPAN, 2 * SPAN), 0)
            - lax.broadcasted_iota(jnp.int32, (SPAN, 2 * SPAN), 1))
    bias_first = jnp.where(dist >= 0, 0.0, -1e30)
    bias_band = jnp.where((dist + SPAN >= 0) & (dist <= 0), 0.0, -1e30)

    for g, (q_ref, k_ref, v_ref) in enumerate(((q0, k0, v0), (q1, k1, v1), (q2, k2, v2))):
        d = DILATIONS[g]
        nb = t // d // SPAN

        def block(idx, carry, g=g, d=d, nb=nb, q_ref=q_ref, k_ref=k_ref, v_ref=v_ref):
            r = idx // nb
            n = idx % nb
            ws = jnp.maximum(n - 1, 0) * SPAN
            q = q_ref[0, r, pl.ds(pl.multiple_of(n * SPAN, SPAN), SPAN), :]
            kw = k_ref[0, r, pl.ds(pl.multiple_of(ws, SPAN), 2 * SPAN), :]
            vw = v_ref[0, r, pl.ds(pl.multiple_of(ws, SPAN), 2 * SPAN), :]
            bias = jnp.where(n == 0, bias_first, bias_band)
            outs, lses = [], []
            vw1 = jnp.concatenate([vw, jnp.ones_like(vw)], axis=1)
            for m_h in (mask_a, mask_b):
                s = _nt_dot(q * m_h, kw) + bias
                m = jnp.max(s, axis=1, keepdims=True)
                p = jnp.exp(s - m)
                od = jnp.dot(p.astype(BF16), vw1, preferred_element_type=F32)
                den = od[:, LANES:]
                outs.append(od[:, :LANES] / den)
                lses.append(m + jnp.log(den))
            o = jnp.where(head_a, outs[0], outs[1])
            lse = jnp.where(head_a, lses[0], lses[1])
            start = r + d * n * SPAN
            rows = pl.ds(start, SPAN) if d == 1 else pl.ds(start, SPAN, stride=d)
            o_scr[g, rows, :] = o
            l_scr[g, rows, :] = lse
            return carry

        lax.fori_loop(0, d * nb, block, 0, unroll=DIL_UNROLL)

    def merge(c, carry):
        rows = pl.ds(pl.multiple_of(c * MERGE_ROWS, MERGE_ROWS), MERGE_ROWS)
        l0, l1, l2 = l_scr[0, rows, :], l_scr[1, rows, :], l_scr[2, rows, :]
        m = jnp.maximum(jnp.maximum(l0, l1), l2)
        e0, e1, e2 = jnp.exp(l0 - m), jnp.exp(l1 - m), jnp.exp(l2 - m)
        num = e0 * o_scr[0, rows, :] + e1 * o_scr[1, rows, :] + e2 * o_scr[2, rows, :]
        o_ref[0, rows, :] = (num / (e0 + e1 + e2)).astype(BF16)
        return carry

    lax.fori_loop(0, t // MERGE_ROWS, merge, 0)


def _dil_attention(g0, g1, g2):
    b, _, t, _ = g0.shape
    assert t % (2 * SPAN * DILATIONS[-1]) == 0 and t % MERGE_ROWS == 0
    n_pairs = PART_W // LANES
    in_specs, args = [], []
    for arr in (g0, g1, g2):
        d, ln = arr.shape[1], arr.shape[2]
        for part in range(3):
            in_specs.append(pl.BlockSpec(
                (1, d, ln, LANES),
                lambda bb, jp, part=part: (bb, 0, 0, part * n_pairs + jp)))
            args.append(arr)
    return pl.pallas_call(
        _dil_kernel,
        grid=(b, n_pairs),
        in_specs=in_specs,
        out_specs=pl.BlockSpec((1, t, LANES), lambda bb, jp: (bb, 0, jp)),
        out_shape=jax.ShapeDtypeStruct((b, t, n_pairs * LANES), BF16),
        scratch_shapes=[pltpu.VMEM((3, t, LANES), F32), pltpu.VMEM((3, t, LANES), F32)],
        compiler_params=_params("parallel", "parallel"),
        name="dilated_attn",
    )(*args)


def _sb_kernel(q_ref, k_ref, v_ref, o_ref):
    tile = SB_TILE
    lane = lax.broadcasted_iota(jnp.int32, (1, LANES), 1)
    head_a = lane < HEAD_DIM
    head_masks = (head_a.astype(BF16), (lane >= HEAD_DIM).astype(BF16))
    row = lax.broadcasted_iota(jnp.int32, (tile, tile), 0)
    col = lax.broadcasted_iota(jnp.int32, (tile, tile), 1)
    tri = (row >= col).astype(BF16)
    tri2 = jnp.concatenate([tri, tri], axis=0)
    past = col < row

    def scores(qh, j, keep):
        k = k_ref[0, pl.ds(pl.multiple_of(j * tile, tile), tile), :]
        z = _nt_dot(qh, k)
        sp = jnp.maximum(z, 0.0) + jnp.log(1.0 + jnp.exp(-jnp.abs(z)))
        if keep is not None:
            sp = jnp.where(keep, sp, 0.0)
        return z, sp

    def weighted(z, sp, j, run, keep):
        v = v_ref[0, pl.ds(pl.multiple_of(j * tile, tile), tile), :]
        hi = sp.astype(BF16)
        lo = (sp - hi.astype(F32)).astype(BF16)
        csum = jnp.dot(jnp.concatenate([hi, lo], axis=1), tri2, preferred_element_type=F32)
        w = jnp.exp(z - csum - run)
        if keep is not None:
            w = jnp.where(keep, w, 0.0)
        return jnp.dot(w.astype(BF16), v, preferred_element_type=F32)

    def live(runs):
        low = jnp.minimum(jnp.min(runs[0]), jnp.min(runs[1]))
        return (low < SB_EXIT).astype(jnp.int32)

    def first_two(i, q_heads):
        has_prev = i > 0
        j_prev = jnp.maximum(i - 1, 0)
        accs, runs = [], []
        for qh in q_heads:
            z0, sp0 = scores(qh, i, past)
            z1, sp1 = scores(qh, j_prev, has_prev)
            run0 = jnp.sum(sp0, axis=1, keepdims=True)
            accs.append(weighted(z0, sp0, i, 0.0, past)
                        + weighted(z1, sp1, j_prev, run0, has_prev))
            runs.append(run0 + jnp.sum(sp1, axis=1, keepdims=True))
        return accs, runs

    def remaining(i, q_heads, accs, runs):
        def step(c):
            j = i - 2 - c[0]
            new_acc, new_run = [], []
            for h, qh in enumerate(q_heads):
                z, sp = scores(qh, j, None)
                new_acc.append(c[2 + h] + weighted(z, sp, j, c[4 + h], None))
                new_run.append(c[4 + h] + jnp.sum(sp, axis=1, keepdims=True))
            return (c[0] + 1, live(new_run), *new_acc, *new_run)

        out = lax.while_loop(lambda c: (c[0] < i - 1) & (c[1] > 0), step,
                             (jnp.int32(0), live(runs), *accs, *runs))
        return jnp.where(head_a, out[2], out[3])

    tiles = []
    for sub in range(SB_QTILES):
        i = pl.program_id(2) * SB_QTILES + sub
        q = q_ref[0, sub * tile:(sub + 1) * tile, :]
        q_heads = tuple(q * m for m in head_masks)
        tiles.append((i, q_heads) + first_two(i, q_heads))
    for sub, (i, q_heads, accs, runs) in enumerate(tiles):
        o_ref[0, sub * tile:(sub + 1) * tile, :] = remaining(i, q_heads, accs, runs).astype(BF16)


def _sb_attention(sb):
    b, t, _ = sb.shape
    n_pairs = N_SB_HEADS * HEAD_DIM // LANES
    rows = SB_TILE * SB_QTILES
    assert t % rows == 0
    kv = lambda part: pl.BlockSpec((1, t, LANES),
                                   lambda bb, jp, i: (bb, 0, part * n_pairs + jp))
    q_tile = pl.BlockSpec((1, rows, LANES), lambda bb, jp, i: (bb, i, jp))
    return pl.pallas_call(
        _sb_kernel,
        grid=(b, n_pairs, t // rows),
        in_specs=[q_tile, kv(1), kv(2)],
        out_specs=q_tile,
        out_shape=jax.ShapeDtypeStruct((b, t, n_pairs * LANES), BF16),
        compiler_params=_params("parallel", "parallel", "arbitrary"),
        name="stick_breaking_attn",
    )(sb, sb, sb)


def _out_kernel(x_ref, gain_ref, od_ref, os_ref, wgd_ref, wgs_ref, wpd_ref, wps_ref, wo_ref,
                o_ref):
    x = x_ref[...]
    h = _rms(x, gain_ref[...]).astype(BF16)
    od, osb = od_ref[...], os_ref[...]
    dot = functools.partial(jnp.dot, preferred_element_type=F32)
    acc = jnp.zeros(x.shape, F32)
    for c in range(x.shape[1] // OUT_CHUNK):
        cols = slice(c * OUT_CHUNK, (c + 1) * OUT_CHUNK)
        y = (_sigmoid(dot(h, _w(wgd_ref, ALL, cols))) * dot(od, _w(wpd_ref, ALL, cols))
             + _sigmoid(dot(h, _w(wgs_ref, ALL, cols))) * dot(osb, _w(wps_ref, ALL, cols)))
        acc = acc + dot(y.astype(BF16), _w(wo_ref, cols, ALL))
    o_ref[...] = x + acc


def _mix_out(x2, gains, o_dil, o_sb, w_in, wpd, wps, wo, l):
    n, d = x2.shape
    assert n % ROW_TILE == 0 and d % OUT_CHUNK == 0 and d == W_IN_BLOCK
    gate_cb = w_in.shape[2] // W_IN_BLOCK - 2
    row = lambda w: pl.BlockSpec((ROW_TILE, w), lambda i: (i, 0))
    return pl.pallas_call(
        _out_kernel,
        grid=(n // ROW_TILE,),
        in_specs=[row(d), _layer(gains, l), row(o_dil.shape[1]), row(o_sb.shape[1]),
                  _layer_cols(w_in, l, gate_cb), _layer_cols(w_in, l, gate_cb + 1),
                  _layer(wpd, l), _layer(wps, l), _layer(wo, l)],
        out_specs=row(d),
        out_shape=jax.ShapeDtypeStruct((n, d), F32),
        compiler_params=_params("parallel"),
        name="mix_out",
    )(x2, gains, o_dil, o_sb, w_in, w_in, wpd, wps, wo)


def _rope_table(t):
    half = ROPE_DIM // 2
    inv_freq = ROPE_THETA ** (-jnp.arange(0, ROPE_DIM, 2, dtype=F32) / ROPE_DIM)
    lane = jnp.arange(LANES) % HEAD_DIM
    ang = jnp.arange(t, dtype=F32)[:, None] * inv_freq[None, :]
    cos = jnp.cos(ang)[:, lane % half]
    sin = jnp.sin(ang)[:, lane % half]
    c = jnp.where(lane < ROPE_DIM, cos, 1.0)
    s1 = jnp.where(lane < half, -sin, 0.0)
    s2 = jnp.where((lane >= half) & (lane < ROPE_DIM), sin, 0.0)
    return jnp.stack([c, s1, s2]).astype(F32)


def kernel(x, norm_ffn1, ffn1_w_gate, ffn1_w_up, ffn1_w_down, norm_mix, w_in,
           w_proj_dil, w_proj_sb, w_out, norm_ffn2, ffn2_w_gate, ffn2_w_up,
           ffn2_w_down, norm_final):
    b, t, dm = x.shape
    depth = w_in.shape[0]
    assert w_in.shape[2] == 3 * D_DIL + 3 * N_SB_HEADS * HEAD_DIM + 2 * dm
    tab = _rope_table(t)
    gain3 = lambda g: g.reshape(-1, 1, dm)
    n1, nm, n2, fin = gain3(norm_ffn1), gain3(norm_mix), gain3(norm_ffn2), gain3(norm_final)

    x2 = x.reshape(b * t, dm)
    for l in range(depth):
        x2 = _ffn(x2, n1, ffn1_w_gate, ffn1_w_up, ffn1_w_down, fin, l, final=False)
        g0, g1, g2, sb = _qkv(x2.reshape(b, t, dm), nm, w_in, tab, l)
        o_dil = _dil_attention(g0, g1, g2)
        o_sb = _sb_attention(sb)
        x2 = _mix_out(x2, nm, o_dil.reshape(b * t, -1), o_sb.reshape(b * t, -1),
                      w_in, w_proj_dil, w_proj_sb, w_out, l)
        x2 = _ffn(x2, n2, ffn2_w_gate, ffn2_w_up, ffn2_w_down, fin, l,
                  final=(l == depth - 1))
    return x2.reshape(b, t, dm)
```

```python
import functools

import jax
import jax.numpy as jnp
from jax import lax
from jax.experimental import pallas as pl
from jax.experimental.pallas import tpu as pltpu

F32 = jnp.float32
BF16 = jnp.bfloat16

HEAD_DIM = 64
DILATIONS = (1, 4, 16)
SPAN = 128
HEADS_PER_GROUP = 4
N_SB_HEADS = 4
ROPE_DIM = HEAD_DIM // 4
ROPE_THETA = 500000.0
RMS_EPS = 1e-6
LANES = 128
PART_W = HEADS_PER_GROUP * HEAD_DIM
GROUP_W = 3 * PART_W
D_DIL = len(DILATIONS) * PART_W
QK_SCALE = HEAD_DIM ** -0.5

ROW_TILE = 1024
FFN_ROWS = 1024
FF_CHUNK = 256
W_IN_BLOCK = 1024
OUT_CHUNK = 256
SB_TILE = 256
SB_QTILES = 4
DIL_UNROLL = True
MERGE_ROWS = 256
SB_EXIT = 105.0

VMEM_LIMIT = 56 * 1024 * 1024
VMEM_PHYSICAL = 64 * 1024 * 1024


def _rms(x, gain):
    ms = jnp.mean(x * x, axis=-1, keepdims=True)
    return x * lax.rsqrt(ms + RMS_EPS) * gain


def _sigmoid(x):
    return 1.0 / (1.0 + jnp.exp(-x))


def _nt_dot(a, b):
    return lax.dot_general(a, b, (((1,), (1,)), ((), ())), preferred_element_type=F32)


def _w(ref, rows, cols):
    return ref[rows, cols].astype(BF16)


ALL = slice(None)


def _layer(arr, l):
    return pl.BlockSpec((None,) + arr.shape[1:], lambda *_: (l, 0, 0),
                        pipeline_mode=pl.Buffered(1))


def _layer_cols(arr, l, cb):
    return pl.BlockSpec((None, arr.shape[1], W_IN_BLOCK), lambda *_: (l, 0, cb),
                        pipeline_mode=pl.Buffered(1))


def _params(*semantics, vmem=VMEM_LIMIT):
    assert vmem < VMEM_PHYSICAL
    return pltpu.CompilerParams(dimension_semantics=semantics, vmem_limit_bytes=vmem)


def _ffn_kernel(x_ref, gain_ref, wg_ref, wu_ref, wd_ref, fin_ref, o_ref, *, final):
    x = x_ref[...]
    h = _rms(x, gain_ref[...]).astype(BF16)
    acc = jnp.zeros(x.shape, F32)
    for c in range(wg_ref.shape[1] // FF_CHUNK):
        cols = slice(c * FF_CHUNK, (c + 1) * FF_CHUNK)
        g = jnp.dot(h, _w(wg_ref, ALL, cols), preferred_element_type=F32)
        u = jnp.dot(h, _w(wu_ref, ALL, cols), preferred_element_type=F32)
        a = (g * _sigmoid(g) * u).astype(BF16)
        acc = acc + jnp.dot(a, _w(wd_ref, cols, ALL), preferred_element_type=F32)
    y = x + 0.5 * acc
    if final:
        y = _rms(y, fin_ref[...])
    o_ref[...] = y


def _ffn(x2, gains, wg, wu, wd, fin_gain, l, *, final):
    n, d = x2.shape
    assert n % FFN_ROWS == 0 and wg.shape[2] % FF_CHUNK == 0
    row = pl.BlockSpec((FFN_ROWS, d), lambda i: (i, 0))
    vmem = 4 * (3 * d * wg.shape[2] + 5 * FFN_ROWS * d) + (4 << 20)
    return pl.pallas_call(
        functools.partial(_ffn_kernel, final=final),
        grid=(n // FFN_ROWS,),
        in_specs=[row, _layer(gains, l), _layer(wg, l), _layer(wu, l), _layer(wd, l),
                  _layer(fin_gain, 0)],
        out_specs=row,
        out_shape=jax.ShapeDtypeStruct((n, d), F32),
        compiler_params=_params("parallel", vmem=vmem),
        name="ffn_final" if final else "ffn",
    )(x2, gains, wg, wu, wd, fin_gain)


def _rope(xs, tab_ref):
    return (xs * tab_ref[0]
            + pltpu.roll(xs, LANES - ROPE_DIM // 2, 1) * tab_ref[1]
            + pltpu.roll(xs, ROPE_DIM // 2, 1) * tab_ref[2])


def _project(hb, w_refs, col0, part_stride, tab_ref, out_ref, d, stage):
    rm = hb.shape[0] // (d or 1)
    blk_w = w_refs[0].shape[1]
    for part in range(3):
        c0 = col0 + part * part_stride
        w_ref, off = w_refs[c0 // blk_w], c0 % blk_w
        p = jnp.dot(hb, _w(w_ref, ALL, slice(off, off + PART_W)), preferred_element_type=F32)
        for half in range(PART_W // LANES):
            xs = p[:, half * LANES:(half + 1) * LANES]
            if part < 2 and tab_ref is not None:
                xs = _rope(xs, tab_ref)
            if part == 0:
                xs = xs * QK_SCALE
            cols = slice(part * PART_W + half * LANES, part * PART_W + (half + 1) * LANES)
            if d is None:
                out_ref[0, :, cols] = xs.astype(BF16)
            elif d == 1:
                out_ref[0, 0, :, cols] = xs.astype(BF16)
            else:
                slot = stage.pop()
                slot[...] = xs
                for r in range(d):
                    out_ref[0, r, :, cols] = slot[pl.ds(r, rm, stride=d), :].astype(BF16)


def _qkv_kernel(x_ref, gain_ref, wa_ref, wb_ref, wc_ref, tab_ref,
                g0_ref, g1_ref, g2_ref, sb_ref, stage_scr):
    hn = _rms(x_ref[0], gain_ref[...]).astype(BF16)
    w_refs = (wa_ref, wb_ref, wc_ref)
    stage = [stage_scr.at[k] for k in range(stage_scr.shape[0])]
    _project(hn, w_refs, 2 * PART_W, D_DIL, tab_ref, g2_ref, DILATIONS[2], stage)
    _project(hn, w_refs, PART_W, D_DIL, tab_ref, g1_ref, DILATIONS[1], stage)
    _project(hn, w_refs, 0, D_DIL, tab_ref, g0_ref, 1, stage)
    _project(hn, w_refs, 3 * D_DIL, PART_W, None, sb_ref, None, stage)


def _qkv(x, gains, w_in, tab, l):
    b, t, dm = x.shape
    tm = ROW_TILE
    d1, d2 = DILATIONS[1], DILATIONS[2]
    qkv_w = 3 * D_DIL + GROUP_W
    assert t % tm == 0 and tm % (8 * d2) == 0 and qkv_w % W_IN_BLOCK == 0
    n_stage = 2 * GROUP_W // LANES
    return pl.pallas_call(
        _qkv_kernel,
        grid=(t // tm, b),
        in_specs=[pl.BlockSpec((1, tm, dm), lambda i, bb: (bb, i, 0)), _layer(gains, l)]
                 + [_layer_cols(w_in, l, cb) for cb in range(qkv_w // W_IN_BLOCK)]
                 + [pl.BlockSpec((3, tm, LANES), lambda i, bb: (0, i, 0))],
        out_specs=[pl.BlockSpec((1, 1, tm, GROUP_W), lambda i, bb: (bb, 0, i, 0)),
                   pl.BlockSpec((1, d1, tm // d1, GROUP_W), lambda i, bb: (bb, 0, i, 0)),
                   pl.BlockSpec((1, d2, tm // d2, GROUP_W), lambda i, bb: (bb, 0, i, 0)),
                   pl.BlockSpec((1, tm, GROUP_W), lambda i, bb: (bb, i, 0))],
        out_shape=[jax.ShapeDtypeStruct((b, 1, t, GROUP_W), BF16),
                   jax.ShapeDtypeStruct((b, d1, t // d1, GROUP_W), BF16),
                   jax.ShapeDtypeStruct((b, d2, t // d2, GROUP_W), BF16),
                   jax.ShapeDtypeStruct((b, t, GROUP_W), BF16)],
        scratch_shapes=[pltpu.VMEM((n_stage, tm, LANES), F32)],
        compiler_params=_params("parallel", "parallel"),
        name="qkv_proj",
    )(x, gains, w_in, w_in, w_in, tab)


def _dil_kernel(q0, k0, v0, q1, k1, v1, q2, k2, v2, o_ref, o_scr, l_scr):
    t = o_ref.shape[1]
    lane = lax.broadcasted_iota(jnp.int32, (1, LANES), 1)
    mask_a = (lane < HEAD_DIM).astype(BF16)
    mask_b = (lane >= HEAD_DIM).astype(BF16)
    head_a = lane < HEAD_DIM
    dist = (lax.broadcasted_iota(jnp.int32, (SPAN, 2 * SPAN), 0)
            - lax.broadcasted_iota(jnp.int32, (SPAN, 2 * SPAN), 1))
    bias_first = jnp.where(dist >= 0, 0.0, -1e30)
    bias_band = jnp.where((dist + SPAN >= 0) & (dist <= 0), 0.0, -1e30)

    for g, (q_ref, k_ref, v_ref) in enumerate(((q0, k0, v0), (q1, k1, v1), (q2, k2, v2))):
        d = DILATIONS[g]
        nb = t // d // SPAN

        def block(idx, carry, g=g, d=d, nb=nb, q_ref=q_ref, k_ref=k_ref, v_ref=v_ref):
            r = idx // nb
            n = idx % nb
            ws = jnp.maximum(n - 1, 0) * SPAN
            q = q_ref[0, r, pl.ds(pl.multiple_of(n * SPAN, SPAN), SPAN), :]
            kw = k_ref[0, r, pl.ds(pl.multiple_of(ws, SPAN), 2 * SPAN), :]
            vw = v_ref[0, r, pl.ds(pl.multiple_of(ws, SPAN), 2 * SPAN), :]
            bias = jnp.where(n == 0, bias_first, bias_band)
            outs, lses = [], []
            vw1 = jnp.concatenate([vw, jnp.ones_like(vw)], axis=1)
            for m_h in (mask_a, mask_b):
                s = _nt_dot(q * m_h, kw) + bias
                m = jnp.max(s, axis=1, keepdims=True)
                p = jnp.exp(s - m)
                od = jnp.dot(p.astype(BF16), vw1, preferred_element_type=F32)
                den = od[:, LANES:]
                outs.append(od[:, :LANES] / den)
                lses.append(m + jnp.log(den))
            o = jnp.where(head_a, outs[0], outs[1])
            lse = jnp.where(head_a, lses[0], lses[1])
            start = r + d * n * SPAN
            rows = pl.ds(start, SPAN) if d == 1 else pl.ds(start, SPAN, stride=d)
            o_scr[g, rows, :] = o
            l_scr[g, rows, :] = lse
            return carry

        lax.fori_loop(0, d * nb, block, 0, unroll=DIL_UNROLL)

    def merge(c, carry):
        rows = pl.ds(pl.multiple_of(c * MERGE_ROWS, MERGE_ROWS), MERGE_ROWS)
        l0, l1, l2 = l_scr[0, rows, :], l_scr[1, rows, :], l_scr[2, rows, :]
        m = jnp.maximum(jnp.maximum(l0, l1), l2)
        e0, e1, e2 = jnp.exp(l0 - m), jnp.exp(l1 - m), jnp.exp(l2 - m)
        num = e0 * o_scr[0, rows, :] + e1 * o_scr[1, rows, :] + e2 * o_scr[2, rows, :]
        o_ref[0, rows, :] = (num / (e0 + e1 + e2)).astype(BF16)
        return carry

    lax.fori_loop(0, t // MERGE_ROWS, merge, 0)


def _dil_attention(g0, g1, g2):
    b, _, t, _ = g0.shape
    assert t % (2 * SPAN * DILATIONS[-1]) == 0 and t % MERGE_ROWS == 0
    n_pairs = PART_W // LANES
    in_specs, args = [], []
    for arr in (g0, g1, g2):
        d, ln = arr.shape[1], arr.shape[2]
        for part in range(3):
            in_specs.append(pl.BlockSpec(
                (1, d, ln, LANES),
                lambda bb, jp, part=part: (bb, 0, 0, part * n_pairs + jp)))
            args.append(arr)
    return pl.pallas_call(
        _dil_kernel,
        grid=(b, n_pairs),
        in_specs=in_specs,
        out_specs=pl.BlockSpec((1, t, LANES), lambda bb, jp: (bb, 0, jp)),
        out_shape=jax.ShapeDtypeStruct((b, t, n_pairs * LANES), BF16),
        scratch_shapes=[pltpu.VMEM((3, t, LANES), F32), pltpu.VMEM((3, t, LANES), F32)],
        compiler_params=_params("parallel", "parallel"),
        name="dilated_attn",
    )(*args)


def _sb_kernel(q_ref, k_ref, v_ref, o_ref):
    tile = SB_TILE
    lane = lax.broadcasted_iota(jnp.int32, (1, LANES), 1)
    head_a = lane < HEAD_DIM
    head_masks = (head_a.astype(BF16), (lane >= HEAD_DIM).astype(BF16))
    row = lax.broadcasted_iota(jnp.int32, (tile, tile), 0)
    col = lax.broadcasted_iota(jnp.int32, (tile, tile), 1)
    tri = (row >= col).astype(BF16)
    tri2 = jnp.concatenate([tri, tri], axis=0)
    past = col < row

    def scores(qh, j, keep):
        k = k_ref[0, pl.ds(pl.multiple_of(j * tile, tile), tile), :]
        z = _nt_dot(qh, k)
        sp = jnp.maximum(z, 0.0) + jnp.log(1.0 + jnp.exp(-jnp.abs(z)))
        if keep is not None:
            sp = jnp.where(keep, sp, 0.0)
        return z, sp

    def weighted(z, sp, j, run, keep):
        v = v_ref[0, pl.ds(pl.multiple_of(j * tile, tile), tile), :]
        hi = sp.astype(BF16)
        lo = (sp - hi.astype(F32)).astype(BF16)
        csum = jnp.dot(jnp.concatenate([hi, lo], axis=1), tri2, preferred_element_type=F32)
        w = jnp.exp(z - csum - run)
        if keep is not None:
            w = jnp.where(keep, w, 0.0)
        return jnp.dot(w.astype(BF16), v, preferred_element_type=F32)

    def live(runs):
        low = jnp.minimum(jnp.min(runs[0]), jnp.min(runs[1]))
        return (low < SB_EXIT).astype(jnp.int32)

    def first_two(i, q_heads):
        has_prev = i > 0
        j_prev = jnp.maximum(i - 1, 0)
        accs, runs = [], []
        for qh in q_heads:
            z0, sp0 = scores(qh, i, past)
            z1, sp1 = scores(qh, j_prev, has_prev)
            run0 = jnp.sum(sp0, axis=1, keepdims=True)
            accs.append(weighted(z0, sp0, i, 0.0, past)
                        + weighted(z1, sp1, j_prev, run0, has_prev))
            runs.append(run0 + jnp.sum(sp1, axis=1, keepdims=True))
        return accs, runs

    def remaining(i, q_heads, accs, runs):
        def step(c):
            j = i - 2 - c[0]
            new_acc, new_run = [], []
            for h, qh in enumerate(q_heads):
                z, sp = scores(qh, j, None)
                new_acc.append(c[2 + h] + weighted(z, sp, j, c[4 + h], None))
                new_run.append(c[4 + h] + jnp.sum(sp, axis=1, keepdims=True))
            return (c[0] + 1, live(new_run), *new_acc, *new_run)

        out = lax.while_loop(lambda c: (c[0] < i - 1) & (c[1] > 0), step,
                             (jnp.int32(0), live(runs), *accs, *runs))
        return jnp.where(head_a, out[2], out[3])

    tiles = []
    for sub in range(SB_QTILES):
        i = pl.program_id(2) * SB_QTILES + sub
        q = q_ref[0, sub * tile:(sub + 1) * tile, :]
        q_heads = tuple(q * m for m in head_masks)
        tiles.append((i, q_heads) + first_two(i, q_heads))
    for sub, (i, q_heads, accs, runs) in enumerate(tiles):
        o_ref[0, sub * tile:(sub + 1) * tile, :] = remaining(i, q_heads, accs, runs).astype(BF16)


def _sb_attention(sb):
    b, t, _ = sb.shape
    n_pairs = N_SB_HEADS * HEAD_DIM // LANES
    rows = SB_TILE * SB_QTILES
    assert t % rows == 0
    kv = lambda part: pl.BlockSpec((1, t, LANES),
                                   lambda bb, jp, i: (bb, 0, part * n_pairs + jp))
    q_tile = pl.BlockSpec((1, rows, LANES), lambda bb, jp, i: (bb, i, jp))
    return pl.pallas_call(
        _sb_kernel,
        grid=(b, n_pairs, t // rows),
        in_specs=[q_tile, kv(1), kv(2)],
        out_specs=q_tile,
        out_shape=jax.ShapeDtypeStruct((b, t, n_pairs * LANES), BF16),
        compiler_params=_params("parallel", "parallel", "arbitrary"),
        name="stick_breaking_attn",
    )(sb, sb, sb)


def _out_kernel(x_ref, gain_ref, od_ref, os_ref, wgd_ref, wgs_ref, wpd_ref, wps_ref, wo_ref,
                o_ref):
    x = x_ref[...]
    h = _rms(x, gain_ref[...]).astype(BF16)
    od, osb = od_ref[...], os_ref[...]
    dot = functools.partial(jnp.dot, preferred_element_type=F32)
    acc = jnp.zeros(x.shape, F32)
    for c in range(x.shape[1] // OUT_CHUNK):
        cols = slice(c * OUT_CHUNK, (c + 1) * OUT_CHUNK)
        y = (_sigmoid(dot(h, _w(wgd_ref, ALL, cols))) * dot(od, _w(wpd_ref, ALL, cols))
             + _sigmoid(dot(h, _w(wgs_ref, ALL, cols))) * dot(osb, _w(wps_ref, ALL, cols)))
        acc = acc + dot(y.astype(BF16), _w(wo_ref, cols, ALL))
    o_ref[...] = x + acc


def _mix_out(x2, gains, o_dil, o_sb, w_in, wpd, wps, wo, l):
    n, d = x2.shape
    assert n % ROW_TILE == 0 and d % OUT_CHUNK == 0 and d == W_IN_BLOCK
    gate_cb = w_in.shape[2] // W_IN_BLOCK - 2
    row = lambda w: pl.BlockSpec((ROW_TILE, w), lambda i: (i, 0))
    return pl.pallas_call(
        _out_kernel,
        grid=(n // ROW_TILE,),
        in_specs=[row(d), _layer(gains, l), row(o_dil.shape[1]), row(o_sb.shape[1]),
                  _layer_cols(w_in, l, gate_cb), _layer_cols(w_in, l, gate_cb + 1),
                  _layer(wpd, l), _layer(wps, l), _layer(wo, l)],
        out_specs=row(d),
        out_shape=jax.ShapeDtypeStruct((n, d), F32),
        compiler_params=_params("parallel"),
        name="mix_out",
    )(x2, gains, o_dil, o_sb, w_in, w_in, wpd, wps, wo)


def _rope_table(t):
    half = ROPE_DIM // 2
    inv_freq = ROPE_THETA ** (-jnp.arange(0, ROPE_DIM, 2, dtype=F32) / ROPE_DIM)
    lane = jnp.arange(LANES) % HEAD_DIM
    ang = jnp.arange(t, dtype=F32)[:, None] * inv_freq[None, :]
    cos = jnp.cos(ang)[:, lane % half]
    sin = jnp.sin(ang)[:, lane % half]
    c = jnp.where(lane < ROPE_DIM, cos, 1.0)
    s1 = jnp.where(lane < half, -sin, 0.0)
    s2 = jnp.where((lane >= half) & (lane < ROPE_DIM), sin, 0.0)
    return jnp.stack([c, s1, s2]).astype(F32)


def kernel(x, norm_ffn1, ffn1_w_gate, ffn1_w_up, ffn1_w_down, norm_mix, w_in,
           w_proj_dil, w_proj_sb, w_out, norm_ffn2, ffn2_w_gate, ffn2_w_up,
           ffn2_w_down, norm_final):
    b, t, dm = x.shape
    depth = w_in.shape[0]
    assert w_in.shape[2] == 3 * D_DIL + 3 * N_SB_HEADS * HEAD_DIM + 2 * dm
    tab = _rope_table(t)
    gain3 = lambda g: g.reshape(-1, 1, dm)
    n1, nm, n2, fin = gain3(norm_ffn1), gain3(norm_mix), gain3(norm_ffn2), gain3(norm_final)

    x2 = x.reshape(b * t, dm)
    for l in range(depth):
        x2 = _ffn(x2, n1, ffn1_w_gate, ffn1_w_up, ffn1_w_down, fin, l, final=False)
        g0, g1, g2, sb = _qkv(x2.reshape(b, t, dm), nm, w_in, tab, l)
        o_dil = _dil_attention(g0, g1, g2)
        o_sb = _sb_attention(sb)
        x2 = _mix_out(x2, nm, o_dil.reshape(b * t, -1), o_sb.reshape(b * t, -1),
                      w_in, w_proj_dil, w_proj_sb, w_out, l)
        x2 = _ffn(x2, n2, ffn2_w_gate, ffn2_w_up, ffn2_w_down, fin, l,
                  final=(l == depth - 1))
    return x2.reshape(b, t, dm)
```

```python
import functools

import jax
import jax.numpy as jnp
from jax import lax
from jax.experimental import pallas as pl
from jax.experimental.pallas import tpu as pltpu

F32 = jnp.float32
BF16 = jnp.bfloat16

HEAD_DIM = 64
DILATIONS = (1, 4, 16)
SPAN = 128
HEADS_PER_GROUP = 4
N_SB_HEADS = 4
ROPE_DIM = HEAD_DIM // 4
ROPE_THETA = 500000.0
RMS_EPS = 1e-6
LANES = 128
PART_W = HEADS_PER_GROUP * HEAD_DIM
GROUP_W = 3 * PART_W
D_DIL = len(DILATIONS) * PART_W
QK_SCALE = HEAD_DIM ** -0.5

ROW_TILE = 1024
FFN_ROWS = 1024
NORM_BLOCKS = 4
FF_CHUNK = 256
W_IN_BLOCK = 1024
OUT_CHUNK = 256
SB_TILE = 256
SB_QTILES = 4
DIL_UNROLL = True
MERGE_ROWS = 256
SB_EXIT = 105.0

VMEM_LIMIT = 56 * 1024 * 1024
VMEM_PHYSICAL = 64 * 1024 * 1024


def _rms(x, gain):
    ms = jnp.mean(x * x, axis=-1, keepdims=True)
    return x * lax.rsqrt(ms + RMS_EPS) * gain


def _sigmoid(x):
    return 1.0 / (1.0 + jnp.exp(-x))


def _nt_dot(a, b):
    return lax.dot_general(a, b, (((1,), (1,)), ((), ())), preferred_element_type=F32)


def _w(ref, rows, cols):
    return ref[rows, cols].astype(BF16)


ALL = slice(None)


def _layer(arr, l):
    return pl.BlockSpec((None,) + arr.shape[1:], lambda *_: (l, 0, 0),
                        pipeline_mode=pl.Buffered(1))


def _layer_cols(arr, l, cb):
    return pl.BlockSpec((None, arr.shape[1], W_IN_BLOCK), lambda *_: (l, 0, cb),
                        pipeline_mode=pl.Buffered(1))


def _params(*semantics, vmem=VMEM_LIMIT):
    assert vmem < VMEM_PHYSICAL
    return pltpu.CompilerParams(dimension_semantics=semantics, vmem_limit_bytes=vmem)


def _ffn_kernel(x_ref, gain_ref, wg_ref, wu_ref, wd_ref, fin_ref, o_ref, *, final):
    x = x_ref[...]
    rb = x.shape[0] // NORM_BLOCKS
    first = slice(0, FF_CHUNK)
    hs, gs, us = [], [], []
    for b in range(NORM_BLOCKS):
        hb = _rms(x[b * rb:(b + 1) * rb], gain_ref[...]).astype(BF16)
        hs.append(hb)
        gs.append(jnp.dot(hb, _w(wg_ref, ALL, first), preferred_element_type=F32))
        us.append(jnp.dot(hb, _w(wu_ref, ALL, first), preferred_element_type=F32))
    h = jnp.concatenate(hs, axis=0)
    acc = jnp.zeros(x.shape, F32)
    for c in range(wg_ref.shape[1] // FF_CHUNK):
        cols = slice(c * FF_CHUNK, (c + 1) * FF_CHUNK)
        if c == 0:
            g, u = jnp.concatenate(gs, axis=0), jnp.concatenate(us, axis=0)
        else:
            g = jnp.dot(h, _w(wg_ref, ALL, cols), preferred_element_type=F32)
            u = jnp.dot(h, _w(wu_ref, ALL, cols), preferred_element_type=F32)
        a = (g * _sigmoid(g) * u).astype(BF16)
        acc = acc + jnp.dot(a, _w(wd_ref, cols, ALL), preferred_element_type=F32)
    y = x + 0.5 * acc
    if final:
        y = _rms(y, fin_ref[...])
    o_ref[...] = y


def _ffn(x2, gains, wg, wu, wd, fin_gain, l, *, final):
    n, d = x2.shape
    assert n % FFN_ROWS == 0 and wg.shape[2] % FF_CHUNK == 0
    row = pl.BlockSpec((FFN_ROWS, d), lambda i: (i, 0))
    vmem = 4 * (3 * d * wg.shape[2] + 5 * FFN_ROWS * d) + (4 << 20)
    return pl.pallas_call(
        functools.partial(_ffn_kernel, final=final),
        grid=(n // FFN_ROWS,),
        in_specs=[row, _layer(gains, l), _layer(wg, l), _layer(wu, l), _layer(wd, l),
                  _layer(fin_gain, 0)],
        out_specs=row,
        out_shape=jax.ShapeDtypeStruct((n, d), F32),
        compiler_params=_params("parallel", vmem=vmem),
        name="ffn_final" if final else "ffn",
    )(x2, gains, wg, wu, wd, fin_gain)


def _rope(xs, tab_ref):
    return (xs * tab_ref[0]
            + pltpu.roll(xs, LANES - ROPE_DIM // 2, 1) * tab_ref[1]
            + pltpu.roll(xs, ROPE_DIM // 2, 1) * tab_ref[2])


def _project(hb, w_refs, col0, part_stride, tab_ref, out_ref, d, stage):
    rm = hb.shape[0] // (d or 1)
    blk_w = w_refs[0].shape[1]
    for part in range(3):
        c0 = col0 + part * part_stride
        w_ref, off = w_refs[c0 // blk_w], c0 % blk_w
        p = jnp.dot(hb, _w(w_ref, ALL, slice(off, off + PART_W)), preferred_element_type=F32)
        for half in range(PART_W // LANES):
            xs = p[:, half * LANES:(half + 1) * LANES]
            if part < 2 and tab_ref is not None:
                xs = _rope(xs, tab_ref)
            if part == 0:
                xs = xs * QK_SCALE
            cols = slice(part * PART_W + half * LANES, part * PART_W + (half + 1) * LANES)
            if d is None:
                out_ref[0, :, cols] = xs.astype(BF16)
            elif d == 1:
                out_ref[0, 0, :, cols] = xs.astype(BF16)
            else:
                slot = stage.pop()
                slot[...] = xs
                for r in range(d):
                    out_ref[0, r, :, cols] = slot[pl.ds(r, rm, stride=d), :].astype(BF16)


def _qkv_kernel(x_ref, gain_ref, wa_ref, wb_ref, wc_ref, tab_ref,
                g0_ref, g1_ref, g2_ref, sb_ref, stage_scr):
    hn = _rms(x_ref[0], gain_ref[...]).astype(BF16)
    w_refs = (wa_ref, wb_ref, wc_ref)
    stage = [stage_scr.at[k] for k in range(stage_scr.shape[0])]
    _project(hn, w_refs, 2 * PART_W, D_DIL, tab_ref, g2_ref, DILATIONS[2], stage)
    _project(hn, w_refs, PART_W, D_DIL, tab_ref, g1_ref, DILATIONS[1], stage)
    _project(hn, w_refs, 0, D_DIL, tab_ref, g0_ref, 1, stage)
    _project(hn, w_refs, 3 * D_DIL, PART_W, None, sb_ref, None, stage)


def _qkv(x, gains, w_in, tab, l):
    b, t, dm = x.shape
    tm = ROW_TILE
    d1, d2 = DILATIONS[1], DILATIONS[2]
    qkv_w = 3 * D_DIL + GROUP_W
    assert t % tm == 0 and tm % (8 * d2) == 0 and qkv_w % W_IN_BLOCK == 0
    n_stage = 2 * GROUP_W // LANES
    return pl.pallas_call(
        _qkv_kernel,
        grid=(t // tm, b),
        in_specs=[pl.BlockSpec((1, tm, dm), lambda i, bb: (bb, i, 0)), _layer(gains, l)]
                 + [_layer_cols(w_in, l, cb) for cb in range(qkv_w // W_IN_BLOCK)]
                 + [pl.BlockSpec((3, tm, LANES), lambda i, bb: (0, i, 0))],
        out_specs=[pl.BlockSpec((1, 1, tm, GROUP_W), lambda i, bb: (bb, 0, i, 0)),
                   pl.BlockSpec((1, d1, tm // d1, GROUP_W), lambda i, bb: (bb, 0, i, 0)),
                   pl.BlockSpec((1, d2, tm // d2, GROUP_W), lambda i, bb: (bb, 0, i, 0)),
                   pl.BlockSpec((1, tm, GROUP_W), lambda i, bb: (bb, i, 0))],
        out_shape=[jax.ShapeDtypeStruct((b, 1, t, GROUP_W), BF16),
                   jax.ShapeDtypeStruct((b, d1, t // d1, GROUP_W), BF16),
                   jax.ShapeDtypeStruct((b, d2, t // d2, GROUP_W), BF16),
                   jax.ShapeDtypeStruct((b, t, GROUP_W), BF16)],
        scratch_shapes=[pltpu.VMEM((n_stage, tm, LANES), F32)],
        compiler_params=_params("parallel", "parallel"),
        name="qkv_proj",
    )(x, gains, w_in, w_in, w_in, tab)


def _dil_kernel(q0, k0, v0, q1, k1, v1, q2, k2, v2, o_ref, o_scr, l_scr):
    t = o_ref.shape[1]
    lane = lax.broadcasted_iota(jnp.int32, (1, LANES), 1)
    mask_a = (lane < HEAD_DIM).astype(BF16)
    mask_b = (lane >= HEAD_DIM).astype(BF16)
    head_a = lane < HEAD_DIM
    dist = (lax.broadcasted_iota(jnp.int32, (SPAN, 2 * SPAN), 0)
            - lax.broadcasted_iota(jnp.int32, (SPAN, 2 * SPAN), 1))
    bias_first = jnp.where(dist >= 0, 0.0, -1e30)
    bias_band = jnp.where((dist + SPAN >= 0) & (dist <= 0), 0.0, -1e30)

    for g, (q_ref, k_ref, v_ref) in enumerate(((q0, k0, v0), (q1, k1, v1), (q2, k2, v2))):
        d = DILATIONS[g]
        nb = t // d // SPAN

        def block(idx, carry, g=g, d=d, nb=nb, q_ref=q_ref, k_ref=k_ref, v_ref=v_ref):
            r = idx // nb
            n = idx % nb
            ws = jnp.maximum(n - 1, 0) * SPAN
            q = q_ref[0, r, pl.ds(pl.multiple_of(n * SPAN, SPAN), SPAN), :]
            kw = k_ref[0, r, pl.ds(pl.multiple_of(ws, SPAN), 2 * SPAN), :]
            vw = v_ref[0, r, pl.ds(pl.multiple_of(ws, SPAN), 2 * SPAN), :]
            bias = jnp.where(n == 0, bias_first, bias_band)
            outs, lses = [], []
            vw1 = jnp.concatenate([vw, jnp.ones_like(vw)], axis=1)
            for m_h in (mask_a, mask_b):
                s = _nt_dot(q * m_h, kw) + bias
                m = jnp.max(s, axis=1, keepdims=True)
                p = jnp.exp(s - m)
                od = jnp.dot(p.astype(BF16), vw1, preferred_element_type=F32)
                den = od[:, LANES:]
                outs.append(od[:, :LANES] / den)
                lses.append(m + jnp.log(den))
            o = jnp.where(head_a, outs[0], outs[1])
            lse = jnp.where(head_a, lses[0], lses[1])
            start = r + d * n * SPAN
            rows = pl.ds(start, SPAN) if d == 1 else pl.ds(start, SPAN, stride=d)
            o_scr[g, rows, :] = o
            l_scr[g, rows, :] = lse
            return carry

        lax.fori_loop(0, d * nb, block, 0, unroll=DIL_UNROLL)

    def merge(c, carry):
        rows = pl.ds(pl.multiple_of(c * MERGE_ROWS, MERGE_ROWS), MERGE_ROWS)
        l0, l1, l2 = l_scr[0, rows, :], l_scr[1, rows, :], l_scr[2, rows, :]
        m = jnp.maximum(jnp.maximum(l0, l1), l2)
        e0, e1, e2 = jnp.exp(l0 - m), jnp.exp(l1 - m), jnp.exp(l2 - m)
        num = e0 * o_scr[0, rows, :] + e1 * o_scr[1, rows, :] + e2 * o_scr[2, rows, :]
        o_ref[0, rows, :] = (num / (e0 + e1 + e2)).astype(BF16)
        return carry

    lax.fori_loop(0, t // MERGE_ROWS, merge, 0)


def _dil_attention(g0, g1, g2):
    b, _, t, _ = g0.shape
    assert t % (2 * SPAN * DILATIONS[-1]) == 0 and t % MERGE_ROWS == 0
    n_pairs = PART_W // LANES
    in_specs, args = [], []
    for arr in (g0, g1, g2):
        d, ln = arr.shape[1], arr.shape[2]
        for part in range(3):
            in_specs.append(pl.BlockSpec(
                (1, d, ln, LANES),
                lambda bb, jp, part=part: (bb, 0, 0, part * n_pairs + jp)))
            args.append(arr)
    return pl.pallas_call(
        _dil_kernel,
        grid=(b, n_pairs),
        in_specs=in_specs,
        out_specs=pl.BlockSpec((1, t, LANES), lambda bb, jp: (bb, 0, jp)),
        out_shape=jax.ShapeDtypeStruct((b, t, n_pairs * LANES), BF16),
        scratch_shapes=[pltpu.VMEM((3, t, LANES), F32), pltpu.VMEM((3, t, LANES), F32)],
        compiler_params=_params("parallel", "parallel"),
        name="dilated_attn",
    )(*args)


def _sb_kernel(q_ref, k_ref, v_ref, o_ref):
    tile = SB_TILE
    lane = lax.broadcasted_iota(jnp.int32, (1, LANES), 1)
    head_a = lane < HEAD_DIM
    head_masks = (head_a.astype(BF16), (lane >= HEAD_DIM).astype(BF16))
    row = lax.broadcasted_iota(jnp.int32, (tile, tile), 0)
    col = lax.broadcasted_iota(jnp.int32, (tile, tile), 1)
    tri = (row >= col).astype(BF16)
    tri2 = jnp.concatenate([tri, tri], axis=0)
    past = col < row

    def scores(qh, j, keep):
        k = k_ref[0, pl.ds(pl.multiple_of(j * tile, tile), tile), :]
        z = _nt_dot(qh, k)
        sp = jnp.maximum(z, 0.0) + jnp.log(1.0 + jnp.exp(-jnp.abs(z)))
        if keep is not None:
            sp = jnp.where(keep, sp, 0.0)
        return z, sp

    def weighted(z, sp, j, run, keep):
        v = v_ref[0, pl.ds(pl.multiple_of(j * tile, tile), tile), :]
        hi = sp.astype(BF16)
        lo = (sp - hi.astype(F32)).astype(BF16)
        csum = jnp.dot(jnp.concatenate([hi, lo], axis=1), tri2, preferred_element_type=F32)
        w = jnp.exp(z - csum - run)
        if keep is not None:
            w = jnp.where(keep, w, 0.0)
        return jnp.dot(w.astype(BF16), v, preferred_element_type=F32)

    def live(runs):
        low = jnp.minimum(jnp.min(runs[0]), jnp.min(runs[1]))
        return (low < SB_EXIT).astype(jnp.int32)

    def tile_scores(i, q_heads):
        has_prev = i > 0
        j_prev = jnp.maximum(i - 1, 0)
        return [(scores(qh, i, past), scores(qh, j_prev, has_prev)) for qh in q_heads]

    def tile_sums(i, scored):
        has_prev = i > 0
        j_prev = jnp.maximum(i - 1, 0)
        accs, runs = [], []
        for (z0, sp0), (z1, sp1) in scored:
            run0 = jnp.sum(sp0, axis=1, keepdims=True)
            accs.append(weighted(z0, sp0, i, 0.0, past)
                        + weighted(z1, sp1, j_prev, run0, has_prev))
            runs.append(run0 + jnp.sum(sp1, axis=1, keepdims=True))
        return accs, runs

    def remaining(i, q_heads, accs, runs):
        def step(c):
            j = i - 2 - c[0]
            new_acc, new_run = [], []
            for h, qh in enumerate(q_heads):
                z, sp = scores(qh, j, None)
                new_acc.append(c[2 + h] + weighted(z, sp, j, c[4 + h], None))
                new_run.append(c[4 + h] + jnp.sum(sp, axis=1, keepdims=True))
            return (c[0] + 1, live(new_run), *new_acc, *new_run)

        out = lax.while_loop(lambda c: (c[0] < i - 1) & (c[1] > 0), step,
                             (jnp.int32(0), live(runs), *accs, *runs))
        return jnp.where(head_a, out[2], out[3])

    ids, heads, scored, sums = [], [], [], []
    for sub in range(SB_QTILES + 1):
        if sub < SB_QTILES:
            i = pl.program_id(2) * SB_QTILES + sub
            q = q_ref[0, sub * tile:(sub + 1) * tile, :]
            ids.append(i)
            heads.append(tuple(q * m for m in head_masks))
            scored.append(tile_scores(i, heads[sub]))
        if sub > 0:
            sums.append(tile_sums(ids[sub - 1], scored[sub - 1]))
    for sub in range(SB_QTILES):
        o_ref[0, sub * tile:(sub + 1) * tile, :] = remaining(
            ids[sub], heads[sub], *sums[sub]).astype(BF16)


def _sb_attention(sb):
    b, t, _ = sb.shape
    n_pairs = N_SB_HEADS * HEAD_DIM // LANES
    rows = SB_TILE * SB_QTILES
    assert t % rows == 0
    kv = lambda part: pl.BlockSpec((1, t, LANES),
                                   lambda bb, jp, i: (bb, 0, part * n_pairs + jp))
    q_tile = pl.BlockSpec((1, rows, LANES), lambda bb, jp, i: (bb, i, jp))
    return pl.pallas_call(
        _sb_kernel,
        grid=(b, n_pairs, t // rows),
        in_specs=[q_tile, kv(1), kv(2)],
        out_specs=q_tile,
        out_shape=jax.ShapeDtypeStruct((b, t, n_pairs * LANES), BF16),
        compiler_params=_params("parallel", "parallel", "arbitrary"),
        name="stick_breaking_attn",
    )(sb, sb, sb)


def _out_kernel(x_ref, gain_ref, od_ref, os_ref, wgd_ref, wgs_ref, wpd_ref, wps_ref, wo_ref,
                o_ref):
    x = x_ref[...]
    h = _rms(x, gain_ref[...]).astype(BF16)
    od, osb = od_ref[...], os_ref[...]
    dot = functools.partial(jnp.dot, preferred_element_type=F32)
    acc = jnp.zeros(x.shape, F32)
    for c in range(x.shape[1] // OUT_CHUNK):
        cols = slice(c * OUT_CHUNK, (c + 1) * OUT_CHUNK)
        y = (_sigmoid(dot(h, _w(wgd_ref, ALL, cols))) * dot(od, _w(wpd_ref, ALL, cols))
             + _sigmoid(dot(h, _w(wgs_ref, ALL, cols))) * dot(osb, _w(wps_ref, ALL, cols)))
        acc = acc + dot(y.astype(BF16), _w(wo_ref, cols, ALL))
    o_ref[...] = x + acc


def _mix_out(x2, gains, o_dil, o_sb, w_in, wpd, wps, wo, l):
    n, d = x2.shape
    assert n % ROW_TILE == 0 and d % OUT_CHUNK == 0 and d == W_IN_BLOCK
    gate_cb = w_in.shape[2] // W_IN_BLOCK - 2
    row = lambda w: pl.BlockSpec((ROW_TILE, w), lambda i: (i, 0))
    return pl.pallas_call(
        _out_kernel,
        grid=(n // ROW_TILE,),
        in_specs=[row(d), _layer(gains, l), row(o_dil.shape[1]), row(o_sb.shape[1]),
                  _layer_cols(w_in, l, gate_cb), _layer_cols(w_in, l, gate_cb + 1),
                  _layer(wpd, l), _layer(wps, l), _layer(wo, l)],
        out_specs=row(d),
        out_shape=jax.ShapeDtypeStruct((n, d), F32),
        compiler_params=_params("parallel"),
        name="mix_out",
    )(x2, gains, o_dil, o_sb, w_in, w_in, wpd, wps, wo)


def _rope_table(t):
    half = ROPE_DIM // 2
    inv_freq = ROPE_THETA ** (-jnp.arange(0, ROPE_DIM, 2, dtype=F32) / ROPE_DIM)
    lane = jnp.arange(LANES) % HEAD_DIM
    ang = jnp.arange(t, dtype=F32)[:, None] * inv_freq[None, :]
    cos = jnp.cos(ang)[:, lane % half]
    sin = jnp.sin(ang)[:, lane % half]
    c = jnp.where(lane < ROPE_DIM, cos, 1.0)
    s1 = jnp.where(lane < half, -sin, 0.0)
    s2 = jnp.where((lane >= half) & (lane < ROPE_DIM), sin, 0.0)
    return jnp.stack([c, s1, s2]).astype(F32)


def kernel(x, norm_ffn1, ffn1_w_gate, ffn1_w_up, ffn1_w_down, norm_mix, w_in,
           w_proj_dil, w_proj_sb, w_out, norm_ffn2, ffn2_w_gate, ffn2_w_up,
           ffn2_w_down, norm_final):
    b, t, dm = x.shape
    depth = w_in.shape[0]
    assert w_in.shape[2] == 3 * D_DIL + 3 * N_SB_HEADS * HEAD_DIM + 2 * dm
    tab = _rope_table(t)
    gain3 = lambda g: g.reshape(-1, 1, dm)
    n1, nm, n2, fin = gain3(norm_ffn1), gain3(norm_mix), gain3(norm_ffn2), gain3(norm_final)

    x2 = x.reshape(b * t, dm)
    for l in range(depth):
        x2 = _ffn(x2, n1, ffn1_w_gate, ffn1_w_up, ffn1_w_down, fin, l, final=False)
        g0, g1, g2, sb = _qkv(x2.reshape(b, t, dm), nm, w_in, tab, l)
        o_dil = _dil_attention(g0, g1, g2)
        o_sb = _sb_attention(sb)
        x2 = _mix_out(x2, nm, o_dil.reshape(b * t, -1), o_sb.reshape(b * t, -1),
                      w_in, w_proj_dil, w_proj_sb, w_out, l)
        x2 = _ffn(x2, n2, ffn2_w_gate, ffn2_w_up, ffn2_w_down, fin, l,
                  final=(l == depth - 1))
    return x2.reshape(b, t, dm)
```

```python
import functools

import jax
import jax.numpy as jnp
from jax import lax
from jax.experimental import pallas as pl
from jax.experimental.pallas import tpu as pltpu

F32 = jnp.float32
BF16 = jnp.bfloat16

HEAD_DIM = 64
DILATIONS = (1, 4, 16)
SPAN = 128
HEADS_PER_GROUP = 4
N_SB_HEADS = 4
ROPE_DIM = HEAD_DIM // 4
ROPE_THETA = 500000.0
RMS_EPS = 1e-6
LANES = 128
PART_W = HEADS_PER_GROUP * HEAD_DIM
GROUP_W = 3 * PART_W
D_DIL = len(DILATIONS) * PART_W
QK_SCALE = HEAD_DIM ** -0.5

ROW_TILE = 1024
FFN_ROWS = 1024
NORM_BLOCKS = 4
FF_CHUNK = 256
W_IN_BLOCK = 1024
OUT_CHUNK = 256
SB_TILE = 256
SB_QTILES = 4
DIL_UNROLL = True
MERGE_ROWS = 256
MASKED = -1e30
SB_EXIT = 105.0

VMEM_LIMIT = 56 * 1024 * 1024
VMEM_PHYSICAL = 64 * 1024 * 1024


def _rms(x, gain):
    ms = jnp.mean(x * x, axis=-1, keepdims=True)
    return x * lax.rsqrt(ms + RMS_EPS) * gain


def _sigmoid(x):
    return 1.0 / (1.0 + jnp.exp(-x))


def _nt_dot(a, b):
    return lax.dot_general(a, b, (((1,), (1,)), ((), ())), preferred_element_type=F32)


def _w(ref, rows, cols):
    return ref[rows, cols].astype(BF16)


ALL = slice(None)


def _layer(arr, l):
    return pl.BlockSpec((None,) + arr.shape[1:], lambda *_: (l, 0, 0),
                        pipeline_mode=pl.Buffered(1))


def _layer_cols(arr, l, cb):
    return pl.BlockSpec((None, arr.shape[1], W_IN_BLOCK), lambda *_: (l, 0, cb),
                        pipeline_mode=pl.Buffered(1))


def _params(*semantics, vmem=VMEM_LIMIT):
    assert vmem < VMEM_PHYSICAL
    return pltpu.CompilerParams(dimension_semantics=semantics, vmem_limit_bytes=vmem)


def _ffn_kernel(x_ref, gain_ref, wg_ref, wu_ref, wd_ref, fin_ref, o_ref, *, final):
    x = x_ref[...]
    rb = x.shape[0] // NORM_BLOCKS
    first = slice(0, FF_CHUNK)
    hs, gs, us = [], [], []
    for b in range(NORM_BLOCKS):
        hb = _rms(x[b * rb:(b + 1) * rb], gain_ref[...]).astype(BF16)
        hs.append(hb)
        gs.append(jnp.dot(hb, _w(wg_ref, ALL, first), preferred_element_type=F32))
        us.append(jnp.dot(hb, _w(wu_ref, ALL, first), preferred_element_type=F32))
    h = jnp.concatenate(hs, axis=0)
    acc = jnp.zeros(x.shape, F32)
    for c in range(wg_ref.shape[1] // FF_CHUNK):
        cols = slice(c * FF_CHUNK, (c + 1) * FF_CHUNK)
        if c == 0:
            g, u = jnp.concatenate(gs, axis=0), jnp.concatenate(us, axis=0)
        else:
            g = jnp.dot(h, _w(wg_ref, ALL, cols), preferred_element_type=F32)
            u = jnp.dot(h, _w(wu_ref, ALL, cols), preferred_element_type=F32)
        a = (g * _sigmoid(g) * u).astype(BF16)
        acc = acc + jnp.dot(a, _w(wd_ref, cols, ALL), preferred_element_type=F32)
    y = x + 0.5 * acc
    if final:
        y = _rms(y, fin_ref[...])
    o_ref[...] = y


def _ffn(x2, gains, wg, wu, wd, fin_gain, l, *, final):
    n, d = x2.shape
    assert n % FFN_ROWS == 0 and wg.shape[2] % FF_CHUNK == 0
    row = pl.BlockSpec((FFN_ROWS, d), lambda i: (i, 0))
    vmem = 4 * (3 * d * wg.shape[2] + 5 * FFN_ROWS * d) + (4 << 20)
    return pl.pallas_call(
        functools.partial(_ffn_kernel, final=final),
        grid=(n // FFN_ROWS,),
        in_specs=[row, _layer(gains, l), _layer(wg, l), _layer(wu, l), _layer(wd, l),
                  _layer(fin_gain, 0)],
        out_specs=row,
        out_shape=jax.ShapeDtypeStruct((n, d), F32),
        compiler_params=_params("parallel", vmem=vmem),
        name="ffn_final" if final else "ffn",
    )(x2, gains, wg, wu, wd, fin_gain)


def _rope(xs, tab_ref):
    return (xs * tab_ref[0]
            + pltpu.roll(xs, LANES - ROPE_DIM // 2, 1) * tab_ref[1]
            + pltpu.roll(xs, ROPE_DIM // 2, 1) * tab_ref[2])


def _project(hb, w_refs, col0, part_stride, tab_ref, out_ref, d, stage):
    rm = hb.shape[0] // (d or 1)
    blk_w = w_refs[0].shape[1]
    for part in range(3):
        c0 = col0 + part * part_stride
        w_ref, off = w_refs[c0 // blk_w], c0 % blk_w
        p = jnp.dot(hb, _w(w_ref, ALL, slice(off, off + PART_W)), preferred_element_type=F32)
        for half in range(PART_W // LANES):
            xs = p[:, half * LANES:(half + 1) * LANES]
            if part < 2 and tab_ref is not None:
                xs = _rope(xs, tab_ref)
            if part == 0:
                xs = xs * QK_SCALE
            cols = slice(part * PART_W + half * LANES, part * PART_W + (half + 1) * LANES)
            if d is None:
                out_ref[0, :, cols] = xs.astype(BF16)
            elif d == 1:
                out_ref[0, 0, :, cols] = xs.astype(BF16)
            else:
                slot = stage.pop()
                slot[...] = xs
                for r in range(d):
                    out_ref[0, r, :, cols] = slot[pl.ds(r, rm, stride=d), :].astype(BF16)


def _qkv_kernel(x_ref, gain_ref, wa_ref, wb_ref, wc_ref, tab_ref,
                g0_ref, g1_ref, g2_ref, sb_ref, stage_scr):
    hn = _rms(x_ref[0], gain_ref[...]).astype(BF16)
    w_refs = (wa_ref, wb_ref, wc_ref)
    stage = [stage_scr.at[k] for k in range(stage_scr.shape[0])]
    _project(hn, w_refs, 2 * PART_W, D_DIL, tab_ref, g2_ref, DILATIONS[2], stage)
    _project(hn, w_refs, PART_W, D_DIL, tab_ref, g1_ref, DILATIONS[1], stage)
    _project(hn, w_refs, 0, D_DIL, tab_ref, g0_ref, 1, stage)
    _project(hn, w_refs, 3 * D_DIL, PART_W, None, sb_ref, None, stage)


def _qkv(x, gains, w_in, tab, l):
    b, t, dm = x.shape
    tm = ROW_TILE
    d1, d2 = DILATIONS[1], DILATIONS[2]
    qkv_w = 3 * D_DIL + GROUP_W
    assert t % tm == 0 and tm % (8 * d2) == 0 and qkv_w % W_IN_BLOCK == 0
    n_stage = 2 * GROUP_W // LANES
    return pl.pallas_call(
        _qkv_kernel,
        grid=(t // tm, b),
        in_specs=[pl.BlockSpec((1, tm, dm), lambda i, bb: (bb, i, 0)), _layer(gains, l)]
                 + [_layer_cols(w_in, l, cb) for cb in range(qkv_w // W_IN_BLOCK)]
                 + [pl.BlockSpec((3, tm, LANES), lambda i, bb: (0, i, 0))],
        out_specs=[pl.BlockSpec((1, 1, tm, GROUP_W), lambda i, bb: (bb, 0, i, 0)),
                   pl.BlockSpec((1, d1, tm // d1, GROUP_W), lambda i, bb: (bb, 0, i, 0)),
                   pl.BlockSpec((1, d2, tm // d2, GROUP_W), lambda i, bb: (bb, 0, i, 0)),
                   pl.BlockSpec((1, tm, GROUP_W), lambda i, bb: (bb, i, 0))],
        out_shape=[jax.ShapeDtypeStruct((b, 1, t, GROUP_W), BF16),
                   jax.ShapeDtypeStruct((b, d1, t // d1, GROUP_W), BF16),
                   jax.ShapeDtypeStruct((b, d2, t // d2, GROUP_W), BF16),
                   jax.ShapeDtypeStruct((b, t, GROUP_W), BF16)],
        scratch_shapes=[pltpu.VMEM((n_stage, tm, LANES), F32)],
        compiler_params=_params("parallel", "parallel"),
        name="qkv_proj",
    )(x, gains, w_in, w_in, w_in, tab)


def _dil_kernel(q0, k0, v0, q1, k1, v1, q2, k2, v2, o_ref, o_scr, l_scr):
    t = o_ref.shape[1]
    lane = lax.broadcasted_iota(jnp.int32, (1, LANES), 1)
    mask_a = (lane < HEAD_DIM).astype(BF16)
    mask_b = (lane >= HEAD_DIM).astype(BF16)
    head_a = lane < HEAD_DIM
    dist = (lax.broadcasted_iota(jnp.int32, (SPAN, 2 * SPAN), 0)
            - lax.broadcasted_iota(jnp.int32, (SPAN, 2 * SPAN), 1))
    bias_first = jnp.where(dist >= 0, 0.0, MASKED)
    bias_band = jnp.where((dist + SPAN >= 0) & (dist <= 0), 0.0, MASKED)

    for g, (q_ref, k_ref, v_ref) in enumerate(((q0, k0, v0), (q1, k1, v1), (q2, k2, v2))):
        d = DILATIONS[g]
        nb = t // d // SPAN

        def block(idx, carry, g=g, d=d, nb=nb, q_ref=q_ref, k_ref=k_ref, v_ref=v_ref):
            r = idx // nb
            n = idx % nb
            ws = jnp.maximum(n - 1, 0) * SPAN
            q = q_ref[0, r, pl.ds(pl.multiple_of(n * SPAN, SPAN), SPAN), :]
            kw = k_ref[0, r, pl.ds(pl.multiple_of(ws, SPAN), 2 * SPAN), :]
            vw = v_ref[0, r, pl.ds(pl.multiple_of(ws, SPAN), 2 * SPAN), :]
            bias = jnp.where(n == 0, bias_first, bias_band)
            outs, lses = [], []
            vw1 = jnp.concatenate([vw, jnp.ones_like(vw)], axis=1)
            for m_h in (mask_a, mask_b):
                s = _nt_dot(q * m_h, kw) + bias
                m = jnp.max(s, axis=1, keepdims=True)
                p = jnp.exp(s - m)
                od = jnp.dot(p.astype(BF16), vw1, preferred_element_type=F32)
                den = od[:, LANES:]
                outs.append(od[:, :LANES] / den)
                lses.append(m + jnp.log(den))
            o = jnp.where(head_a, outs[0], outs[1])
            lse = jnp.where(head_a, lses[0], lses[1])
            start = r + d * n * SPAN
            rows = pl.ds(start, SPAN) if d == 1 else pl.ds(start, SPAN, stride=d)
            o_scr[g, rows, :] = o
            l_scr[g, rows, :] = lse
            return carry

        lax.fori_loop(0, d * nb, block, 0, unroll=DIL_UNROLL)

    def merge(c, carry):
        rows = pl.ds(pl.multiple_of(c * MERGE_ROWS, MERGE_ROWS), MERGE_ROWS)
        l0, l1, l2 = l_scr[0, rows, :], l_scr[1, rows, :], l_scr[2, rows, :]
        m = jnp.maximum(jnp.maximum(l0, l1), l2)
        e0, e1, e2 = jnp.exp(l0 - m), jnp.exp(l1 - m), jnp.exp(l2 - m)
        num = e0 * o_scr[0, rows, :] + e1 * o_scr[1, rows, :] + e2 * o_scr[2, rows, :]
        o_ref[0, rows, :] = (num / (e0 + e1 + e2)).astype(BF16)
        return carry

    lax.fori_loop(0, t // MERGE_ROWS, merge, 0)


def _dil_attention(g0, g1, g2):
    b, _, t, _ = g0.shape
    assert t % (2 * SPAN * DILATIONS[-1]) == 0 and t % MERGE_ROWS == 0
    n_pairs = PART_W // LANES
    in_specs, args = [], []
    for arr in (g0, g1, g2):
        d, ln = arr.shape[1], arr.shape[2]
        for part in range(3):
            in_specs.append(pl.BlockSpec(
                (1, d, ln, LANES),
                lambda bb, jp, part=part: (bb, 0, 0, part * n_pairs + jp)))
            args.append(arr)
    return pl.pallas_call(
        _dil_kernel,
        grid=(b, n_pairs),
        in_specs=in_specs,
        out_specs=pl.BlockSpec((1, t, LANES), lambda bb, jp: (bb, 0, jp)),
        out_shape=jax.ShapeDtypeStruct((b, t, n_pairs * LANES), BF16),
        scratch_shapes=[pltpu.VMEM((3, t, LANES), F32), pltpu.VMEM((3, t, LANES), F32)],
        compiler_params=_params("parallel", "parallel"),
        name="dilated_attn",
    )(*args)


def _sb_kernel(q_ref, k_ref, v_ref, o_ref):
    tile = SB_TILE
    lane = lax.broadcasted_iota(jnp.int32, (1, LANES), 1)
    head_a = lane < HEAD_DIM
    head_masks = (head_a.astype(BF16), (lane >= HEAD_DIM).astype(BF16))
    row = lax.broadcasted_iota(jnp.int32, (tile, tile), 0)
    col = lax.broadcasted_iota(jnp.int32, (tile, tile), 1)
    tri = (row >= col).astype(BF16)
    tri2 = jnp.concatenate([tri, tri], axis=0)
    past = col < row

    def scores(qh, j, keep):
        k = k_ref[0, pl.ds(pl.multiple_of(j * tile, tile), tile), :]
        z = _nt_dot(qh, k)
        sp = jnp.maximum(z, 0.0) + jnp.log(1.0 + jnp.exp(-jnp.abs(z)))
        if keep is not None:
            sp = jnp.where(keep, sp, 0.0)
        return z, sp

    def values(j):
        return v_ref[0, pl.ds(pl.multiple_of(j * tile, tile), tile), :]

    def cumsum(sp):
        hi = sp.astype(BF16)
        lo = (sp - hi.astype(F32)).astype(BF16)
        return jnp.dot(jnp.concatenate([hi, lo], axis=1), tri2, preferred_element_type=F32)

    def weights(z, csum, run, keep):
        w = jnp.exp(z - csum - run)
        if keep is not None:
            w = jnp.where(keep, w, 0.0)
        return w.astype(BF16)

    def weighted(z, sp, j, run, keep):
        return jnp.dot(weights(z, cumsum(sp), run, keep), values(j),
                       preferred_element_type=F32)

    def live(runs):
        low = jnp.minimum(jnp.min(runs[0]), jnp.min(runs[1]))
        return (low < SB_EXIT).astype(jnp.int32)

    def tile_scores(i, q_heads):
        has_prev = i > 0
        j_prev = jnp.maximum(i - 1, 0)
        return [(scores(qh, i, past), scores(qh, j_prev, has_prev)) for qh in q_heads]

    def tile_sums(i, scored):
        has_prev = i > 0
        j_prev = jnp.maximum(i - 1, 0)
        csums = [(cumsum(sp0), cumsum(sp1)) for (_, sp0), (_, sp1) in scored]
        v0, v1 = values(i), values(j_prev)
        accs, runs = [], []
        for ((z0, sp0), (z1, sp1)), (c0, c1) in zip(scored, csums):
            run0 = jnp.sum(sp0, axis=1, keepdims=True)
            w0 = weights(z0, c0, 0.0, past)
            w1 = weights(z1, c1, run0, has_prev)
            accs.append(jnp.dot(w0, v0, preferred_element_type=F32)
                        + jnp.dot(w1, v1, preferred_element_type=F32))
            runs.append(run0 + jnp.sum(sp1, axis=1, keepdims=True))
        return accs, runs

    def remaining(i, q_heads, accs, runs):
        def step(c):
            j = i - 2 - c[0]
            new_acc, new_run = [], []
            for h, qh in enumerate(q_heads):
                z, sp = scores(qh, j, None)
                new_acc.append(c[2 + h] + weighted(z, sp, j, c[4 + h], None))
                new_run.append(c[4 + h] + jnp.sum(sp, axis=1, keepdims=True))
            return (c[0] + 1, live(new_run), *new_acc, *new_run)

        out = lax.while_loop(lambda c: (c[0] < i - 1) & (c[1] > 0), step,
                             (jnp.int32(0), live(runs), *accs, *runs))
        return jnp.where(head_a, out[2], out[3])

    ids, heads, scored, sums = [], [], [], []
    for sub in range(SB_QTILES + 1):
        if sub < SB_QTILES:
            i = pl.program_id(2) * SB_QTILES + sub
            q = q_ref[0, sub * tile:(sub + 1) * tile, :]
            ids.append(i)
            heads.append(tuple(q * m for m in head_masks))
            scored.append(tile_scores(i, heads[sub]))
        if sub > 0:
            sums.append(tile_sums(ids[sub - 1], scored[sub - 1]))
    for sub in range(SB_QTILES):
        o_ref[0, sub * tile:(sub + 1) * tile, :] = remaining(
            ids[sub], heads[sub], *sums[sub]).astype(BF16)


def _sb_attention(sb):
    b, t, _ = sb.shape
    n_pairs = N_SB_HEADS * HEAD_DIM // LANES
    rows = SB_TILE * SB_QTILES
    assert t % rows == 0
    kv = lambda part: pl.BlockSpec((1, t, LANES),
                                   lambda bb, jp, i: (bb, 0, part * n_pairs + jp))
    q_tile = pl.BlockSpec((1, rows, LANES), lambda bb, jp, i: (bb, i, jp))
    return pl.pallas_call(
        _sb_kernel,
        grid=(b, n_pairs, t // rows),
        in_specs=[q_tile, kv(1), kv(2)],
        out_specs=q_tile,
        out_shape=jax.ShapeDtypeStruct((b, t, n_pairs * LANES), BF16),
        compiler_params=_params("parallel", "parallel", "arbitrary"),
        name="stick_breaking_attn",
    )(sb, sb, sb)


def _out_kernel(x_ref, gain_ref, od_ref, os_ref, wgd_ref, wgs_ref, wpd_ref, wps_ref, wo_ref,
                o_ref):
    x = x_ref[...]
    h = _rms(x, gain_ref[...]).astype(BF16)
    od, osb = od_ref[...], os_ref[...]
    dot = functools.partial(jnp.dot, preferred_element_type=F32)
    acc = jnp.zeros(x.shape, F32)
    for c in range(x.shape[1] // OUT_CHUNK):
        cols = slice(c * OUT_CHUNK, (c + 1) * OUT_CHUNK)
        y = (_sigmoid(dot(h, _w(wgd_ref, ALL, cols))) * dot(od, _w(wpd_ref, ALL, cols))
             + _sigmoid(dot(h, _w(wgs_ref, ALL, cols))) * dot(osb, _w(wps_ref, ALL, cols)))
        acc = acc + dot(y.astype(BF16), _w(wo_ref, cols, ALL))
    o_ref[...] = x + acc


def _mix_out(x2, gains, o_dil, o_sb, w_in, wpd, wps, wo, l):
    n, d = x2.shape
    assert n % ROW_TILE == 0 and d % OUT_CHUNK == 0 and d == W_IN_BLOCK
    gate_cb = w_in.shape[2] // W_IN_BLOCK - 2
    row = lambda w: pl.BlockSpec((ROW_TILE, w), lambda i: (i, 0))
    return pl.pallas_call(
        _out_kernel,
        grid=(n // ROW_TILE,),
        in_specs=[row(d), _layer(gains, l), row(o_dil.shape[1]), row(o_sb.shape[1]),
                  _layer_cols(w_in, l, gate_cb), _layer_cols(w_in, l, gate_cb + 1),
                  _layer(wpd, l), _layer(wps, l), _layer(wo, l)],
        out_specs=row(d),
        out_shape=jax.ShapeDtypeStruct((n, d), F32),
        compiler_params=_params("parallel"),
        name="mix_out",
    )(x2, gains, o_dil, o_sb, w_in, w_in, wpd, wps, wo)


def _rope_table(t):
    half = ROPE_DIM // 2
    inv_freq = ROPE_THETA ** (-jnp.arange(0, ROPE_DIM, 2, dtype=F32) / ROPE_DIM)
    lane = jnp.arange(LANES) % HEAD_DIM
    ang = jnp.arange(t, dtype=F32)[:, None] * inv_freq[None, :]
    cos = jnp.cos(ang)[:, lane % half]
    sin = jnp.sin(ang)[:, lane % half]
    c = jnp.where(lane < ROPE_DIM, cos, 1.0)
    s1 = jnp.where(lane < half, -sin, 0.0)
    s2 = jnp.where((lane >= half) & (lane < ROPE_DIM), sin, 0.0)
    return jnp.stack([c, s1, s2]).astype(F32)


def kernel(x, norm_ffn1, ffn1_w_gate, ffn1_w_up, ffn1_w_down, norm_mix, w_in,
           w_proj_dil, w_proj_sb, w_out, norm_ffn2, ffn2_w_gate, ffn2_w_up,
           ffn2_w_down, norm_final):
    b, t, dm = x.shape
    depth = w_in.shape[0]
    assert w_in.shape[2] == 3 * D_DIL + 3 * N_SB_HEADS * HEAD_DIM + 2 * dm
    tab = _rope_table(t)
    gain3 = lambda g: g.reshape(-1, 1, dm)
    n1, nm, n2, fin = gain3(norm_ffn1), gain3(norm_mix), gain3(norm_ffn2), gain3(norm_final)

    x2 = x.reshape(b * t, dm)
    for l in range(depth):
        x2 = _ffn(x2, n1, ffn1_w_gate, ffn1_w_up, ffn1_w_down, fin, l, final=False)
        g0, g1, g2, sb = _qkv(x2.reshape(b, t, dm), nm, w_in, tab, l)
        o_dil = _dil_attention(g0, g1, g2)
        o_sb = _sb_attention(sb)
        x2 = _mix_out(x2, nm, o_dil.reshape(b * t, -1), o_sb.reshape(b * t, -1),
                      w_in, w_proj_dil, w_proj_sb, w_out, l)
        x2 = _ffn(x2, n2, ffn2_w_gate, ffn2_w_up, ffn2_w_down, fin, l,
                  final=(l == depth - 1))
    return x2.reshape(b, t, dm)
```

```python
import functools

import jax
import jax.numpy as jnp
from jax import lax
from jax.experimental import pallas as pl
from jax.experimental.pallas import tpu as pltpu

F32 = jnp.float32
BF16 = jnp.bfloat16

HEAD_DIM = 64
DILATIONS = (1, 4, 16)
SPAN = 128
HEADS_PER_GROUP = 4
N_SB_HEADS = 4
ROPE_DIM = HEAD_DIM // 4
ROPE_THETA = 500000.0
RMS_EPS = 1e-6
LANES = 128
PART_W = HEADS_PER_GROUP * HEAD_DIM
GROUP_W = 3 * PART_W
D_DIL = len(DILATIONS) * PART_W
QK_SCALE = HEAD_DIM ** -0.5

ROW_TILE = 1024
FFN_ROWS = 1024
NORM_BLOCKS = 4
FF_CHUNK = 256
W_IN_BLOCK = 1024
OUT_CHUNK = 256
SB_TILE = 256
SB_QTILES = 4
DIL_UNROLL = True
MERGE_ROWS = 256
MASKED = -1e30
SB_EXIT = 105.0

VMEM_LIMIT = 56 * 1024 * 1024
VMEM_PHYSICAL = 64 * 1024 * 1024


def _rms(x, gain):
    ms = jnp.mean(x * x, axis=-1, keepdims=True)
    return x * lax.rsqrt(ms + RMS_EPS) * gain


def _sigmoid(x):
    return 1.0 / (1.0 + jnp.exp(-x))


def _nt_dot(a, b):
    return lax.dot_general(a, b, (((1,), (1,)), ((), ())), preferred_element_type=F32)


def _w(ref, rows, cols):
    return ref[rows, cols].astype(BF16)


ALL = slice(None)


def _layer(arr, l):
    return pl.BlockSpec((None,) + arr.shape[1:], lambda *_: (l, 0, 0),
                        pipeline_mode=pl.Buffered(1))


def _layer_cols(arr, l, cb):
    return pl.BlockSpec((None, arr.shape[1], W_IN_BLOCK), lambda *_: (l, 0, cb),
                        pipeline_mode=pl.Buffered(1))


def _params(*semantics, vmem=VMEM_LIMIT):
    assert vmem < VMEM_PHYSICAL
    return pltpu.CompilerParams(dimension_semantics=semantics, vmem_limit_bytes=vmem)


def _ffn_kernel(x_ref, gain_ref, wg_ref, wu_ref, wd_ref, fin_ref, o_ref, *, final):
    x = x_ref[...]
    rb = x.shape[0] // NORM_BLOCKS
    first = slice(0, FF_CHUNK)
    hs, gs, us = [], [], []
    for b in range(NORM_BLOCKS):
        hb = _rms(x[b * rb:(b + 1) * rb], gain_ref[...]).astype(BF16)
        hs.append(hb)
        gs.append(jnp.dot(hb, _w(wg_ref, ALL, first), preferred_element_type=F32))
        us.append(jnp.dot(hb, _w(wu_ref, ALL, first), preferred_element_type=F32))
    h = jnp.concatenate(hs, axis=0)
    acc = jnp.zeros(x.shape, F32)
    for c in range(wg_ref.shape[1] // FF_CHUNK):
        cols = slice(c * FF_CHUNK, (c + 1) * FF_CHUNK)
        if c == 0:
            g, u = jnp.concatenate(gs, axis=0), jnp.concatenate(us, axis=0)
        else:
            g = jnp.dot(h, _w(wg_ref, ALL, cols), preferred_element_type=F32)
            u = jnp.dot(h, _w(wu_ref, ALL, cols), preferred_element_type=F32)
        a = (g * _sigmoid(g) * u).astype(BF16)
        acc = acc + jnp.dot(a, _w(wd_ref, cols, ALL), preferred_element_type=F32)
    y = x + 0.5 * acc
    if final:
        y = _rms(y, fin_ref[...])
    o_ref[...] = y


def _ffn(x2, gains, wg, wu, wd, fin_gain, l, *, final):
    n, d = x2.shape
    assert n % FFN_ROWS == 0 and wg.shape[2] % FF_CHUNK == 0
    row = pl.BlockSpec((FFN_ROWS, d), lambda i: (i, 0))
    vmem = 4 * (3 * d * wg.shape[2] + 5 * FFN_ROWS * d) + (4 << 20)
    return pl.pallas_call(
        functools.partial(_ffn_kernel, final=final),
        grid=(n // FFN_ROWS,),
        in_specs=[row, _layer(gains, l), _layer(wg, l), _layer(wu, l), _layer(wd, l),
                  _layer(fin_gain, 0)],
        out_specs=row,
        out_shape=jax.ShapeDtypeStruct((n, d), F32),
        compiler_params=_params("parallel", vmem=vmem),
        name="ffn_final" if final else "ffn",
    )(x2, gains, wg, wu, wd, fin_gain)


def _rope(xs, tab_ref):
    return (xs * tab_ref[0]
            + pltpu.roll(xs, LANES - ROPE_DIM // 2, 1) * tab_ref[1]
            + pltpu.roll(xs, ROPE_DIM // 2, 1) * tab_ref[2])


def _project(hb, w_refs, col0, part_stride, tab_ref, out_ref, d, stage, first=()):
    rm = hb.shape[0] // (d or 1)
    blk_w = w_refs[0].shape[1]
    for part in range(3):
        c0 = col0 + part * part_stride
        w_ref, off = w_refs[c0 // blk_w], c0 % blk_w
        if part < len(first):
            p = first[part]
        else:
            p = jnp.dot(hb, _w(w_ref, ALL, slice(off, off + PART_W)),
                        preferred_element_type=F32)
        for half in range(PART_W // LANES):
            xs = p[:, half * LANES:(half + 1) * LANES]
            if part < 2 and tab_ref is not None:
                xs = _rope(xs, tab_ref)
            if part == 0:
                xs = xs * QK_SCALE
            cols = slice(part * PART_W + half * LANES, part * PART_W + (half + 1) * LANES)
            if d is None:
                out_ref[0, :, cols] = xs.astype(BF16)
            elif d == 1:
                out_ref[0, 0, :, cols] = xs.astype(BF16)
            else:
                slot = stage.pop()
                slot[...] = xs
                for r in range(d):
                    out_ref[0, r, :, cols] = slot[pl.ds(r, rm, stride=d), :].astype(BF16)


def _qkv_kernel(x_ref, gain_ref, wa_ref, wb_ref, wc_ref, tab_ref,
                g0_ref, g1_ref, g2_ref, sb_ref, stage_scr):
    w_refs = (wa_ref, wb_ref, wc_ref)
    stage = [stage_scr.at[k] for k in range(stage_scr.shape[0])]
    x = x_ref[0]
    rb = x.shape[0] // NORM_BLOCKS
    hs, qs, ks = [], [], []
    for b in range(NORM_BLOCKS):
        hb = _rms(x[b * rb:(b + 1) * rb], gain_ref[...]).astype(BF16)
        hs.append(hb)
        qs.append(jnp.dot(hb, _w(wa_ref, ALL, slice(0, PART_W)), preferred_element_type=F32))
        ks.append(jnp.dot(hb, _w(wa_ref, ALL, slice(D_DIL, D_DIL + PART_W)),
                          preferred_element_type=F32))
    hn = jnp.concatenate(hs, axis=0)
    first = [jnp.concatenate(qs, axis=0), jnp.concatenate(ks, axis=0)]
    _project(hn, w_refs, 0, D_DIL, tab_ref, g0_ref, 1, stage, first)
    _project(hn, w_refs, 2 * PART_W, D_DIL, tab_ref, g2_ref, DILATIONS[2], stage)
    _project(hn, w_refs, PART_W, D_DIL, tab_ref, g1_ref, DILATIONS[1], stage)
    _project(hn, w_refs, 3 * D_DIL, PART_W, None, sb_ref, None, stage)


def _qkv(x, gains, w_in, tab, l):
    b, t, dm = x.shape
    tm = ROW_TILE
    d1, d2 = DILATIONS[1], DILATIONS[2]
    qkv_w = 3 * D_DIL + GROUP_W
    assert t % tm == 0 and tm % (8 * d2) == 0 and qkv_w % W_IN_BLOCK == 0
    n_stage = 2 * GROUP_W // LANES
    return pl.pallas_call(
        _qkv_kernel,
        grid=(t // tm, b),
        in_specs=[pl.BlockSpec((1, tm, dm), lambda i, bb: (bb, i, 0)), _layer(gains, l)]
                 + [_layer_cols(w_in, l, cb) for cb in range(qkv_w // W_IN_BLOCK)]
                 + [pl.BlockSpec((3, tm, LANES), lambda i, bb: (0, i, 0))],
        out_specs=[pl.BlockSpec((1, 1, tm, GROUP_W), lambda i, bb: (bb, 0, i, 0)),
                   pl.BlockSpec((1, d1, tm // d1, GROUP_W), lambda i, bb: (bb, 0, i, 0)),
                   pl.BlockSpec((1, d2, tm // d2, GROUP_W), lambda i, bb: (bb, 0, i, 0)),
                   pl.BlockSpec((1, tm, GROUP_W), lambda i, bb: (bb, i, 0))],
        out_shape=[jax.ShapeDtypeStruct((b, 1, t, GROUP_W), BF16),
                   jax.ShapeDtypeStruct((b, d1, t // d1, GROUP_W), BF16),
                   jax.ShapeDtypeStruct((b, d2, t // d2, GROUP_W), BF16),
                   jax.ShapeDtypeStruct((b, t, GROUP_W), BF16)],
        scratch_shapes=[pltpu.VMEM((n_stage, tm, LANES), F32)],
        compiler_params=_params("parallel", "parallel"),
        name="qkv_proj",
    )(x, gains, w_in, w_in, w_in, tab)


def _dil_kernel(q0, k0, v0, q1, k1, v1, q2, k2, v2, o_ref, o_scr, l_scr):
    t = o_ref.shape[1]
    lane = lax.broadcasted_iota(jnp.int32, (1, LANES), 1)
    mask_a = (lane < HEAD_DIM).astype(BF16)
    mask_b = (lane >= HEAD_DIM).astype(BF16)
    head_a = lane < HEAD_DIM
    dist = (lax.broadcasted_iota(jnp.int32, (SPAN, 2 * SPAN), 0)
            - lax.broadcasted_iota(jnp.int32, (SPAN, 2 * SPAN), 1))
    bias_first = jnp.where(dist >= 0, 0.0, MASKED)
    bias_band = jnp.where((dist + SPAN >= 0) & (dist <= 0), 0.0, MASKED)

    for g, (q_ref, k_ref, v_ref) in enumerate(((q0, k0, v0), (q1, k1, v1), (q2, k2, v2))):
        d = DILATIONS[g]
        nb = t // d // SPAN

        def block(idx, carry, g=g, d=d, nb=nb, q_ref=q_ref, k_ref=k_ref, v_ref=v_ref):
            r = idx // nb
            n = idx % nb
            ws = jnp.maximum(n - 1, 0) * SPAN
            q = q_ref[0, r, pl.ds(pl.multiple_of(n * SPAN, SPAN), SPAN), :]
            kw = k_ref[0, r, pl.ds(pl.multiple_of(ws, SPAN), 2 * SPAN), :]
            vw = v_ref[0, r, pl.ds(pl.multiple_of(ws, SPAN), 2 * SPAN), :]
            bias = jnp.where(n == 0, bias_first, bias_band)
            outs, lses = [], []
            vw1 = jnp.concatenate([vw, jnp.ones_like(vw)], axis=1)
            for m_h in (mask_a, mask_b):
                s = _nt_dot(q * m_h, kw) + bias
                m = jnp.max(s, axis=1, keepdims=True)
                p = jnp.exp(s - m)
                od = jnp.dot(p.astype(BF16), vw1, preferred_element_type=F32)
                den = od[:, LANES:]
                outs.append(od[:, :LANES] / den)
                lses.append(m + jnp.log(den))
            o = jnp.where(head_a, outs[0], outs[1])
            lse = jnp.where(head_a, lses[0], lses[1])
            start = r + d * n * SPAN
            rows = pl.ds(start, SPAN) if d == 1 else pl.ds(start, SPAN, stride=d)
            o_scr[g, rows, :] = o
            l_scr[g, rows, :] = lse
            return carry

        lax.fori_loop(0, d * nb, block, 0, unroll=DIL_UNROLL)

    def merge(c, carry):
        rows = pl.ds(pl.multiple_of(c * MERGE_ROWS, MERGE_ROWS), MERGE_ROWS)
        l0, l1, l2 = l_scr[0, rows, :], l_scr[1, rows, :], l_scr[2, rows, :]
        m = jnp.maximum(jnp.maximum(l0, l1), l2)
        e0, e1, e2 = jnp.exp(l0 - m), jnp.exp(l1 - m), jnp.exp(l2 - m)
        num = e0 * o_scr[0, rows, :] + e1 * o_scr[1, rows, :] + e2 * o_scr[2, rows, :]
        o_ref[0, rows, :] = (num / (e0 + e1 + e2)).astype(BF16)
        return carry

    lax.fori_loop(0, t // MERGE_ROWS, merge, 0)


def _dil_attention(g0, g1, g2):
    b, _, t, _ = g0.shape
    assert t % (2 * SPAN * DILATIONS[-1]) == 0 and t % MERGE_ROWS == 0
    n_pairs = PART_W // LANES
    in_specs, args = [], []
    for arr in (g0, g1, g2):
        d, ln = arr.shape[1], arr.shape[2]
        for part in range(3):
            in_specs.append(pl.BlockSpec(
                (1, d, ln, LANES),
                lambda bb, jp, part=part: (bb, 0, 0, part * n_pairs + jp)))
            args.append(arr)
    return pl.pallas_call(
        _dil_kernel,
        grid=(b, n_pairs),
        in_specs=in_specs,
        out_specs=pl.BlockSpec((1, t, LANES), lambda bb, jp: (bb, 0, jp)),
        out_shape=jax.ShapeDtypeStruct((b, t, n_pairs * LANES), BF16),
        scratch_shapes=[pltpu.VMEM((3, t, LANES), F32), pltpu.VMEM((3, t, LANES), F32)],
        compiler_params=_params("parallel", "parallel"),
        name="dilated_attn",
    )(*args)


def _sb_kernel(q_ref, k_ref, v_ref, o_ref):
    tile = SB_TILE
    lane = lax.broadcasted_iota(jnp.int32, (1, LANES), 1)
    head_a = lane < HEAD_DIM
    head_masks = (head_a.astype(BF16), (lane >= HEAD_DIM).astype(BF16))
    row = lax.broadcasted_iota(jnp.int32, (tile, tile), 0)
    col = lax.broadcasted_iota(jnp.int32, (tile, tile), 1)
    tri = (row >= col).astype(BF16)
    tri2 = jnp.concatenate([tri, tri], axis=0)
    past = col < row

    def scores(qh, j, keep):
        k = k_ref[0, pl.ds(pl.multiple_of(j * tile, tile), tile), :]
        z = _nt_dot(qh, k)
        sp = jnp.maximum(z, 0.0) + jnp.log(1.0 + jnp.exp(-jnp.abs(z)))
        if keep is not None:
            sp = jnp.where(keep, sp, 0.0)
        return z, sp

    def values(j):
        return v_ref[0, pl.ds(pl.multiple_of(j * tile, tile), tile), :]

    def cumsum(sp):
        hi = sp.astype(BF16)
        lo = (sp - hi.astype(F32)).astype(BF16)
        return jnp.dot(jnp.concatenate([hi, lo], axis=1), tri2, preferred_element_type=F32)

    def weights(z, csum, run, keep):
        w = jnp.exp(z - csum - run)
        if keep is not None:
            w = jnp.where(keep, w, 0.0)
        return w.astype(BF16)

    def weighted(z, sp, j, run, keep):
        return jnp.dot(weights(z, cumsum(sp), run, keep), values(j),
                       preferred_element_type=F32)

    def live(runs):
        low = jnp.minimum(jnp.min(runs[0]), jnp.min(runs[1]))
        return (low < SB_EXIT).astype(jnp.int32)

    def tile_scores(i, q_heads):
        has_prev = i > 0
        j_prev = jnp.maximum(i - 1, 0)
        return [(scores(qh, i, past), scores(qh, j_prev, has_prev)) for qh in q_heads]

    def tile_sums(i, scored):
        has_prev = i > 0
        j_prev = jnp.maximum(i - 1, 0)
        csums = [(cumsum(sp0), cumsum(sp1)) for (_, sp0), (_, sp1) in scored]
        v0, v1 = values(i), values(j_prev)
        accs, runs = [], []
        for ((z0, sp0), (z1, sp1)), (c0, c1) in zip(scored, csums):
            run0 = jnp.sum(sp0, axis=1, keepdims=True)
            w0 = weights(z0, c0, 0.0, past)
            w1 = weights(z1, c1, run0, has_prev)
            accs.append(jnp.dot(w0, v0, preferred_element_type=F32)
                        + jnp.dot(w1, v1, preferred_element_type=F32))
            runs.append(run0 + jnp.sum(sp1, axis=1, keepdims=True))
        return accs, runs

    def remaining(i, q_heads, accs, runs):
        def step(c):
            j = i - 2 - c[0]
            new_acc, new_run = [], []
            for h, qh in enumerate(q_heads):
                z, sp = scores(qh, j, None)
                new_acc.append(c[2 + h] + weighted(z, sp, j, c[4 + h], None))
                new_run.append(c[4 + h] + jnp.sum(sp, axis=1, keepdims=True))
            return (c[0] + 1, live(new_run), *new_acc, *new_run)

        out = lax.while_loop(lambda c: (c[0] < i - 1) & (c[1] > 0), step,
                             (jnp.int32(0), live(runs), *accs, *runs))
        return jnp.where(head_a, out[2], out[3])

    ids, heads, scored, sums = [], [], [], []
    for sub in range(SB_QTILES + 1):
        if sub < SB_QTILES:
            i = pl.program_id(2) * SB_QTILES + sub
            q = q_ref[0, sub * tile:(sub + 1) * tile, :]
            ids.append(i)
            heads.append(tuple(q * m for m in head_masks))
            scored.append(tile_scores(i, heads[sub]))
        if sub > 0:
            sums.append(tile_sums(ids[sub - 1], scored[sub - 1]))
    for sub in range(SB_QTILES):
        o_ref[0, sub * tile:(sub + 1) * tile, :] = remaining(
            ids[sub], heads[sub], *sums[sub]).astype(BF16)


def _sb_attention(sb):
    b, t, _ = sb.shape
    n_pairs = N_SB_HEADS * HEAD_DIM // LANES
    rows = SB_TILE * SB_QTILES
    assert t % rows == 0
    kv = lambda part: pl.BlockSpec((1, t, LANES),
                                   lambda bb, jp, i: (bb, 0, part * n_pairs + jp))
    q_tile = pl.BlockSpec((1, rows, LANES), lambda bb, jp, i: (bb, i, jp))
    return pl.pallas_call(
        _sb_kernel,
        grid=(b, n_pairs, t // rows),
        in_specs=[q_tile, kv(1), kv(2)],
        out_specs=q_tile,
        out_shape=jax.ShapeDtypeStruct((b, t, n_pairs * LANES), BF16),
        compiler_params=_params("parallel", "parallel", "arbitrary"),
        name="stick_breaking_attn",
    )(sb, sb, sb)


def _out_kernel(x_ref, gain_ref, od_ref, os_ref, wgd_ref, wgs_ref, wpd_ref, wps_ref, wo_ref,
                o_ref):
    x = x_ref[...]
    h = _rms(x, gain_ref[...]).astype(BF16)
    od, osb = od_ref[...], os_ref[...]
    dot = functools.partial(jnp.dot, preferred_element_type=F32)
    acc = jnp.zeros(x.shape, F32)
    for c in range(x.shape[1] // OUT_CHUNK):
        cols = slice(c * OUT_CHUNK, (c + 1) * OUT_CHUNK)
        y = (_sigmoid(dot(h, _w(wgd_ref, ALL, cols))) * dot(od, _w(wpd_ref, ALL, cols))
             + _sigmoid(dot(h, _w(wgs_ref, ALL, cols))) * dot(osb, _w(wps_ref, ALL, cols)))
        acc = acc + dot(y.astype(BF16), _w(wo_ref, cols, ALL))
    o_ref[...] = x + acc


def _mix_out(x2, gains, o_dil, o_sb, w_in, wpd, wps, wo, l):
    n, d = x2.shape
    assert n % ROW_TILE == 0 and d % OUT_CHUNK == 0 and d == W_IN_BLOCK
    gate_cb = w_in.shape[2] // W_IN_BLOCK - 2
    row = lambda w: pl.BlockSpec((ROW_TILE, w), lambda i: (i, 0))
    return pl.pallas_call(
        _out_kernel,
        grid=(n // ROW_TILE,),
        in_specs=[row(d), _layer(gains, l), row(o_dil.shape[1]), row(o_sb.shape[1]),
                  _layer_cols(w_in, l, gate_cb), _layer_cols(w_in, l, gate_cb + 1),
                  _layer(wpd, l), _layer(wps, l), _layer(wo, l)],
        out_specs=row(d),
        out_shape=jax.ShapeDtypeStruct((n, d), F32),
        compiler_params=_params("parallel"),
        name="mix_out",
    )(x2, gains, o_dil, o_sb, w_in, w_in, wpd, wps, wo)


def _rope_table(t):
    half = ROPE_DIM // 2
    inv_freq = ROPE_THETA ** (-jnp.arange(0, ROPE_DIM, 2, dtype=F32) / ROPE_DIM)
    lane = jnp.arange(LANES) % HEAD_DIM
    ang = jnp.arange(t, dtype=F32)[:, None] * inv_freq[None, :]
    cos = jnp.cos(ang)[:, lane % half]
    sin = jnp.sin(ang)[:, lane % half]
    c = jnp.where(lane < ROPE_DIM, cos, 1.0)
    s1 = jnp.where(lane < half, -sin, 0.0)
    s2 = jnp.where((lane >= half) & (lane < ROPE_DIM), sin, 0.0)
    return jnp.stack([c, s1, s2]).astype(F32)


def kernel(x, norm_ffn1, ffn1_w_gate, ffn1_w_up, ffn1_w_down, norm_mix, w_in,
           w_proj_dil, w_proj_sb, w_out, norm_ffn2, ffn2_w_gate, ffn2_w_up,
           ffn2_w_down, norm_final):
    b, t, dm = x.shape
    depth = w_in.shape[0]
    assert w_in.shape[2] == 3 * D_DIL + 3 * N_SB_HEADS * HEAD_DIM + 2 * dm
    tab = _rope_table(t)
    gain3 = lambda g: g.reshape(-1, 1, dm)
    n1, nm, n2, fin = gain3(norm_ffn1), gain3(norm_mix), gain3(norm_ffn2), gain3(norm_final)

    x2 = x.reshape(b * t, dm)
    for l in range(depth):
        x2 = _ffn(x2, n1, ffn1_w_gate, ffn1_w_up, ffn1_w_down, fin, l, final=False)
        g0, g1, g2, sb = _qkv(x2.reshape(b, t, dm), nm, w_in, tab, l)
        o_dil = _dil_attention(g0, g1, g2)
        o_sb = _sb_attention(sb)
        x2 = _mix_out(x2, nm, o_dil.reshape(b * t, -1), o_sb.reshape(b * t, -1),
                      w_in, w_proj_dil, w_proj_sb, w_out, l)
        x2 = _ffn(x2, n2, ffn2_w_gate, ffn2_w_up, ffn2_w_down, fin, l,
                  final=(l == depth - 1))
    return x2.reshape(b, t, dm)
```

```python
import functools

import jax
import jax.numpy as jnp
from jax import lax
from jax.experimental import pallas as pl
from jax.experimental.pallas import tpu as pltpu

F32 = jnp.float32
BF16 = jnp.bfloat16

HEAD_DIM = 64
DILATIONS = (1, 4, 16)
SPAN = 128
HEADS_PER_GROUP = 4
N_SB_HEADS = 4
ROPE_DIM = HEAD_DIM // 4
ROPE_THETA = 500000.0
RMS_EPS = 1e-6
LANES = 128
PART_W = HEADS_PER_GROUP * HEAD_DIM
GROUP_W = 3 * PART_W
D_DIL = len(DILATIONS) * PART_W
QK_SCALE = HEAD_DIM ** -0.5

ROW_TILE = 1024
FFN_ROWS = 1024
NORM_BLOCKS = 4
FF_CHUNK = 256
W_IN_BLOCK = 1024
OUT_CHUNK = 256
SB_TILE = 256
SB_QTILES = 4
DIL_UNROLL = True
MERGE_ROWS = 256
MASKED = -1e30
SB_EXIT = float("inf")

VMEM_LIMIT = 56 * 1024 * 1024
VMEM_PHYSICAL = 64 * 1024 * 1024


def _rms(x, gain):
    ms = jnp.mean(x * x, axis=-1, keepdims=True)
    return x * lax.rsqrt(ms + RMS_EPS) * gain


def _sigmoid(x):
    return 1.0 / (1.0 + jnp.exp(-x))


def _nt_dot(a, b):
    return lax.dot_general(a, b, (((1,), (1,)), ((), ())), preferred_element_type=F32)


def _w(ref, rows, cols):
    return ref[rows, cols].astype(BF16)


ALL = slice(None)


def _layer(arr, l):
    return pl.BlockSpec((None,) + arr.shape[1:], lambda *_: (l, 0, 0),
                        pipeline_mode=pl.Buffered(1))


def _layer_cols(arr, l, cb):
    return pl.BlockSpec((None, arr.shape[1], W_IN_BLOCK), lambda *_: (l, 0, cb),
                        pipeline_mode=pl.Buffered(1))


def _params(*semantics, vmem=VMEM_LIMIT):
    assert vmem < VMEM_PHYSICAL
    return pltpu.CompilerParams(dimension_semantics=semantics, vmem_limit_bytes=vmem)


def _ffn_kernel(x_ref, gain_ref, wg_ref, wu_ref, wd_ref, fin_ref, o_ref, *, final):
    x = x_ref[...]
    rb = x.shape[0] // NORM_BLOCKS
    first = slice(0, FF_CHUNK)
    hs, gs, us = [], [], []
    for b in range(NORM_BLOCKS):
        hb = _rms(x[b * rb:(b + 1) * rb], gain_ref[...]).astype(BF16)
        hs.append(hb)
        gs.append(jnp.dot(hb, _w(wg_ref, ALL, first), preferred_element_type=F32))
        us.append(jnp.dot(hb, _w(wu_ref, ALL, first), preferred_element_type=F32))
    h = jnp.concatenate(hs, axis=0)
    acc = jnp.zeros(x.shape, F32)
    for c in range(wg_ref.shape[1] // FF_CHUNK):
        cols = slice(c * FF_CHUNK, (c + 1) * FF_CHUNK)
        if c == 0:
            g, u = jnp.concatenate(gs, axis=0), jnp.concatenate(us, axis=0)
        else:
            g = jnp.dot(h, _w(wg_ref, ALL, cols), preferred_element_type=F32)
            u = jnp.dot(h, _w(wu_ref, ALL, cols), preferred_element_type=F32)
        a = (g * _sigmoid(g) * u).astype(BF16)
        acc = acc + jnp.dot(a, _w(wd_ref, cols, ALL), preferred_element_type=F32)
    y = x + 0.5 * acc
    if final:
        y = _rms(y, fin_ref[...])
    o_ref[...] = y


def _ffn(x2, gains, wg, wu, wd, fin_gain, l, *, final):
    n, d = x2.shape
    assert n % FFN_ROWS == 0 and wg.shape[2] % FF_CHUNK == 0
    row = pl.BlockSpec((FFN_ROWS, d), lambda i: (i, 0))
    vmem = 4 * (3 * d * wg.shape[2] + 5 * FFN_ROWS * d) + (4 << 20)
    return pl.pallas_call(
        functools.partial(_ffn_kernel, final=final),
        grid=(n // FFN_ROWS,),
        in_specs=[row, _layer(gains, l), _layer(wg, l), _layer(wu, l), _layer(wd, l),
                  _layer(fin_gain, 0)],
        out_specs=row,
        out_shape=jax.ShapeDtypeStruct((n, d), F32),
        compiler_params=_params("parallel", vmem=vmem),
        name="ffn_final" if final else "ffn",
    )(x2, gains, wg, wu, wd, fin_gain)


def _rope(xs, tab_ref):
    return (xs * tab_ref[0]
            + pltpu.roll(xs, LANES - ROPE_DIM // 2, 1) * tab_ref[1]
            + pltpu.roll(xs, ROPE_DIM // 2, 1) * tab_ref[2])


def _project(hb, w_refs, col0, part_stride, tab_ref, out_ref, d, stage, first=()):
    rm = hb.shape[0] // (d or 1)
    blk_w = w_refs[0].shape[1]
    for part in range(3):
        c0 = col0 + part * part_stride
        w_ref, off = w_refs[c0 // blk_w], c0 % blk_w
        if part < len(first):
            p = first[part]
        else:
            p = jnp.dot(hb, _w(w_ref, ALL, slice(off, off + PART_W)),
                        preferred_element_type=F32)
        for half in range(PART_W // LANES):
            xs = p[:, half * LANES:(half + 1) * LANES]
            if part < 2 and tab_ref is not None:
                xs = _rope(xs, tab_ref)
            if part == 0:
                xs = xs * QK_SCALE
            cols = slice(part * PART_W + half * LANES, part * PART_W + (half + 1) * LANES)
            if d is None:
                out_ref[0, :, cols] = xs.astype(BF16)
            elif d == 1:
                out_ref[0, 0, :, cols] = xs.astype(BF16)
            else:
                slot = stage.pop()
                slot[...] = xs
                for r in range(d):
                    out_ref[0, r, :, cols] = slot[pl.ds(r, rm, stride=d), :].astype(BF16)


def _qkv_kernel(x_ref, gain_ref, wa_ref, wb_ref, wc_ref, tab_ref,
                g0_ref, g1_ref, g2_ref, sb_ref, stage_scr):
    w_refs = (wa_ref, wb_ref, wc_ref)
    stage = [stage_scr.at[k] for k in range(stage_scr.shape[0])]
    x = x_ref[0]
    rb = x.shape[0] // NORM_BLOCKS
    hs, qs, ks = [], [], []
    for b in range(NORM_BLOCKS):
        hb = _rms(x[b * rb:(b + 1) * rb], gain_ref[...]).astype(BF16)
        hs.append(hb)
        qs.append(jnp.dot(hb, _w(wa_ref, ALL, slice(0, PART_W)), preferred_element_type=F32))
        ks.append(jnp.dot(hb, _w(wa_ref, ALL, slice(D_DIL, D_DIL + PART_W)),
                          preferred_element_type=F32))
    hn = jnp.concatenate(hs, axis=0)
    first = [jnp.concatenate(qs, axis=0), jnp.concatenate(ks, axis=0)]
    _project(hn, w_refs, 0, D_DIL, tab_ref, g0_ref, 1, stage, first)
    _project(hn, w_refs, 2 * PART_W, D_DIL, tab_ref, g2_ref, DILATIONS[2], stage)
    _project(hn, w_refs, PART_W, D_DIL, tab_ref, g1_ref, DILATIONS[1], stage)
    _project(hn, w_refs, 3 * D_DIL, PART_W, None, sb_ref, None, stage)


def _qkv(x, gains, w_in, tab, l):
    b, t, dm = x.shape
    tm = ROW_TILE
    d1, d2 = DILATIONS[1], DILATIONS[2]
    qkv_w = 3 * D_DIL + GROUP_W
    assert t % tm == 0 and tm % (8 * d2) == 0 and qkv_w % W_IN_BLOCK == 0
    n_stage = 2 * GROUP_W // LANES
    return pl.pallas_call(
        _qkv_kernel,
        grid=(t // tm, b),
        in_specs=[pl.BlockSpec((1, tm, dm), lambda i, bb: (bb, i, 0)), _layer(gains, l)]
                 + [_layer_cols(w_in, l, cb) for cb in range(qkv_w // W_IN_BLOCK)]
                 + [pl.BlockSpec((3, tm, LANES), lambda i, bb: (0, i, 0))],
        out_specs=[pl.BlockSpec((1, 1, tm, GROUP_W), lambda i, bb: (bb, 0, i, 0)),
                   pl.BlockSpec((1, d1, tm // d1, GROUP_W), lambda i, bb: (bb, 0, i, 0)),
                   pl.BlockSpec((1, d2, tm // d2, GROUP_W), lambda i, bb: (bb, 0, i, 0)),
                   pl.BlockSpec((1, tm, GROUP_W), lambda i, bb: (bb, i, 0))],
        out_shape=[jax.ShapeDtypeStruct((b, 1, t, GROUP_W), BF16),
                   jax.ShapeDtypeStruct((b, d1, t // d1, GROUP_W), BF16),
                   jax.ShapeDtypeStruct((b, d2, t // d2, GROUP_W), BF16),
                   jax.ShapeDtypeStruct((b, t, GROUP_W), BF16)],
        scratch_shapes=[pltpu.VMEM((n_stage, tm, LANES), F32)],
        compiler_params=_params("parallel", "parallel"),
        name="qkv_proj",
    )(x, gains, w_in, w_in, w_in, tab)


def _dil_kernel(q0, k0, v0, q1, k1, v1, q2, k2, v2, o_ref, o_scr, l_scr):
    t = o_ref.shape[1]
    lane = lax.broadcasted_iota(jnp.int32, (1, LANES), 1)
    mask_a = (lane < HEAD_DIM).astype(BF16)
    mask_b = (lane >= HEAD_DIM).astype(BF16)
    head_a = lane < HEAD_DIM
    dist = (lax.broadcasted_iota(jnp.int32, (SPAN, 2 * SPAN), 0)
            - lax.broadcasted_iota(jnp.int32, (SPAN, 2 * SPAN), 1))
    bias_first = jnp.where(dist >= 0, 0.0, MASKED)
    bias_band = jnp.where((dist + SPAN >= 0) & (dist <= 0), 0.0, MASKED)

    for g, (q_ref, k_ref, v_ref) in enumerate(((q0, k0, v0), (q1, k1, v1), (q2, k2, v2))):
        d = DILATIONS[g]
        nb = t // d // SPAN

        def block(idx, carry, g=g, d=d, nb=nb, q_ref=q_ref, k_ref=k_ref, v_ref=v_ref):
            r = idx // nb
            n = idx % nb
            ws = jnp.maximum(n - 1, 0) * SPAN
            q = q_ref[0, r, pl.ds(pl.multiple_of(n * SPAN, SPAN), SPAN), :]
            kw = k_ref[0, r, pl.ds(pl.multiple_of(ws, SPAN), 2 * SPAN), :]
            vw = v_ref[0, r, pl.ds(pl.multiple_of(ws, SPAN), 2 * SPAN), :]
            bias = jnp.where(n == 0, bias_first, bias_band)
            outs, lses = [], []
            vw1 = jnp.concatenate([vw, jnp.ones_like(vw)], axis=1)
            for m_h in (mask_a, mask_b):
                s = _nt_dot(q * m_h, kw) + bias
                m = jnp.max(s, axis=1, keepdims=True)
                p = jnp.exp(s - m)
                od = jnp.dot(p.astype(BF16), vw1, preferred_element_type=F32)
                den = od[:, LANES:]
                outs.append(od[:, :LANES] / den)
                lses.append(m + jnp.log(den))
            o = jnp.where(head_a, outs[0], outs[1])
            lse = jnp.where(head_a, lses[0], lses[1])
            start = r + d * n * SPAN
            rows = pl.ds(start, SPAN) if d == 1 else pl.ds(start, SPAN, stride=d)
            o_scr[g, rows, :] = o
            l_scr[g, rows, :] = lse
            return carry

        lax.fori_loop(0, d * nb, block, 0, unroll=DIL_UNROLL)

    def merge(c, carry):
        rows = pl.ds(pl.multiple_of(c * MERGE_ROWS, MERGE_ROWS), MERGE_ROWS)
        l0, l1, l2 = l_scr[0, rows, :], l_scr[1, rows, :], l_scr[2, rows, :]
        m = jnp.maximum(jnp.maximum(l0, l1), l2)
        e0, e1, e2 = jnp.exp(l0 - m), jnp.exp(l1 - m), jnp.exp(l2 - m)
        num = e0 * o_scr[0, rows, :] + e1 * o_scr[1, rows, :] + e2 * o_scr[2, rows, :]
        o_ref[0, rows, :] = (num / (e0 + e1 + e2)).astype(BF16)
        return carry

    lax.fori_loop(0, t // MERGE_ROWS, merge, 0)


def _dil_attention(g0, g1, g2):
    b, _, t, _ = g0.shape
    assert t % (2 * SPAN * DILATIONS[-1]) == 0 and t % MERGE_ROWS == 0
    n_pairs = PART_W // LANES
    in_specs, args = [], []
    for arr in (g0, g1, g2):
        d, ln = arr.shape[1], arr.shape[2]
        for part in range(3):
            in_specs.append(pl.BlockSpec(
                (1, d, ln, LANES),
                lambda bb, jp, part=part: (bb, 0, 0, part * n_pairs + jp)))
            args.append(arr)
    return pl.pallas_call(
        _dil_kernel,
        grid=(b, n_pairs),
        in_specs=in_specs,
        out_specs=pl.BlockSpec((1, t, LANES), lambda bb, jp: (bb, 0, jp)),
        out_shape=jax.ShapeDtypeStruct((b, t, n_pairs * LANES), BF16),
        scratch_shapes=[pltpu.VMEM((3, t, LANES), F32), pltpu.VMEM((3, t, LANES), F32)],
        compiler_params=_params("parallel", "parallel"),
        name="dilated_attn",
    )(*args)


def _sb_kernel(q_ref, k_ref, v_ref, o_ref):
    tile = SB_TILE
    lane = lax.broadcasted_iota(jnp.int32, (1, LANES), 1)
    head_a = lane < HEAD_DIM
    head_masks = (head_a.astype(BF16), (lane >= HEAD_DIM).astype(BF16))
    row = lax.broadcasted_iota(jnp.int32, (tile, tile), 0)
    col = lax.broadcasted_iota(jnp.int32, (tile, tile), 1)
    tri = (row >= col).astype(BF16)
    tri2 = jnp.concatenate([tri, tri], axis=0)
    past = col < row

    def scores(qh, j, keep):
        k = k_ref[0, pl.ds(pl.multiple_of(j * tile, tile), tile), :]
        z = _nt_dot(qh, k)
        sp = jnp.maximum(z, 0.0) + jnp.log(1.0 + jnp.exp(-jnp.abs(z)))
        if keep is not None:
            sp = jnp.where(keep, sp, 0.0)
        return z, sp

    def values(j):
        return v_ref[0, pl.ds(pl.multiple_of(j * tile, tile), tile), :]

    def cumsum(sp):
        hi = sp.astype(BF16)
        lo = (sp - hi.astype(F32)).astype(BF16)
        return jnp.dot(jnp.concatenate([hi, lo], axis=1), tri2, preferred_element_type=F32)

    def weights(z, csum, run, keep):
        w = jnp.exp(z - csum - run)
        if keep is not None:
            w = jnp.where(keep, w, 0.0)
        return w.astype(BF16)

    def weighted(z, sp, j, run, keep):
        return jnp.dot(weights(z, cumsum(sp), run, keep), values(j),
                       preferred_element_type=F32)

    def live(runs):
        low = jnp.minimum(jnp.min(runs[0]), jnp.min(runs[1]))
        return (low < SB_EXIT).astype(jnp.int32)

    def tile_scores(i, q_heads):
        has_prev = i > 0
        j_prev = jnp.maximum(i - 1, 0)
        return [(scores(qh, i, past), scores(qh, j_prev, has_prev)) for qh in q_heads]

    def tile_sums(i, scored):
        has_prev = i > 0
        j_prev = jnp.maximum(i - 1, 0)
        csums = [(cumsum(sp0), cumsum(sp1)) for (_, sp0), (_, sp1) in scored]
        v0, v1 = values(i), values(j_prev)
        accs, runs = [], []
        for ((z0, sp0), (z1, sp1)), (c0, c1) in zip(scored, csums):
            run0 = jnp.sum(sp0, axis=1, keepdims=True)
            w0 = weights(z0, c0, 0.0, past)
            w1 = weights(z1, c1, run0, has_prev)
            accs.append(jnp.dot(w0, v0, preferred_element_type=F32)
                        + jnp.dot(w1, v1, preferred_element_type=F32))
            runs.append(run0 + jnp.sum(sp1, axis=1, keepdims=True))
        return accs, runs

    def remaining(i, q_heads, accs, runs):
        def step(c):
            j = i - 2 - c[0]
            new_acc, new_run = [], []
            for h, qh in enumerate(q_heads):
                z, sp = scores(qh, j, None)
                new_acc.append(c[2 + h] + weighted(z, sp, j, c[4 + h], None))
                new_run.append(c[4 + h] + jnp.sum(sp, axis=1, keepdims=True))
            return (c[0] + 1, live(new_run), *new_acc, *new_run)

        out = lax.while_loop(lambda c: (c[0] < i - 1) & (c[1] > 0), step,
                             (jnp.int32(0), live(runs), *accs, *runs))
        return jnp.where(head_a, out[2], out[3])

    ids, heads, scored, sums = [], [], [], []
    for sub in range(SB_QTILES + 1):
        if sub < SB_QTILES:
            i = pl.program_id(2) * SB_QTILES + sub
            q = q_ref[0, sub * tile:(sub + 1) * tile, :]
            ids.append(i)
            heads.append(tuple(q * m for m in head_masks))
            scored.append(tile_scores(i, heads[sub]))
        if sub > 0:
            sums.append(tile_sums(ids[sub - 1], scored[sub - 1]))
    for sub in range(SB_QTILES):
        o_ref[0, sub * tile:(sub + 1) * tile, :] = remaining(
            ids[sub], heads[sub], *sums[sub]).astype(BF16)


def _sb_attention(sb):
    b, t, _ = sb.shape
    n_pairs = N_SB_HEADS * HEAD_DIM // LANES
    rows = SB_TILE * SB_QTILES
    assert t % rows == 0
    kv = lambda part: pl.BlockSpec((1, t, LANES),
                                   lambda bb, jp, i: (bb, 0, part * n_pairs + jp))
    q_tile = pl.BlockSpec((1, rows, LANES), lambda bb, jp, i: (bb, i, jp))
    return pl.pallas_call(
        _sb_kernel,
        grid=(b, n_pairs, t // rows),
        in_specs=[q_tile, kv(1), kv(2)],
        out_specs=q_tile,
        out_shape=jax.ShapeDtypeStruct((b, t, n_pairs * LANES), BF16),
        compiler_params=_params("parallel", "parallel", "arbitrary"),
        name="stick_breaking_attn",
    )(sb, sb, sb)


def _out_kernel(x_ref, gain_ref, od_ref, os_ref, wgd_ref, wgs_ref, wpd_ref, wps_ref, wo_ref,
                o_ref):
    x = x_ref[...]
    h = _rms(x, gain_ref[...]).astype(BF16)
    od, osb = od_ref[...], os_ref[...]
    dot = functools.partial(jnp.dot, preferred_element_type=F32)
    acc = jnp.zeros(x.shape, F32)
    for c in range(x.shape[1] // OUT_CHUNK):
        cols = slice(c * OUT_CHUNK, (c + 1) * OUT_CHUNK)
        y = (_sigmoid(dot(h, _w(wgd_ref, ALL, cols))) * dot(od, _w(wpd_ref, ALL, cols))
             + _sigmoid(dot(h, _w(wgs_ref, ALL, cols))) * dot(osb, _w(wps_ref, ALL, cols)))
        acc = acc + dot(y.astype(BF16), _w(wo_ref, cols, ALL))
    o_ref[...] = x + acc


def _mix_out(x2, gains, o_dil, o_sb, w_in, wpd, wps, wo, l):
    n, d = x2.shape
    assert n % ROW_TILE == 0 and d % OUT_CHUNK == 0 and d == W_IN_BLOCK
    gate_cb = w_in.shape[2] // W_IN_BLOCK - 2
    row = lambda w: pl.BlockSpec((ROW_TILE, w), lambda i: (i, 0))
    return pl.pallas_call(
        _out_kernel,
        grid=(n // ROW_TILE,),
        in_specs=[row(d), _layer(gains, l), row(o_dil.shape[1]), row(o_sb.shape[1]),
                  _layer_cols(w_in, l, gate_cb), _layer_cols(w_in, l, gate_cb + 1),
                  _layer(wpd, l), _layer(wps, l), _layer(wo, l)],
        out_specs=row(d),
        out_shape=jax.ShapeDtypeStruct((n, d), F32),
        compiler_params=_params("parallel"),
        name="mix_out",
    )(x2, gains, o_dil, o_sb, w_in, w_in, wpd, wps, wo)


def _rope_table(t):
    half = ROPE_DIM // 2
    inv_freq = ROPE_THETA ** (-jnp.arange(0, ROPE_DIM, 2, dtype=F32) / ROPE_DIM)
    lane = jnp.arange(LANES) % HEAD_DIM
    ang = jnp.arange(t, dtype=F32)[:, None] * inv_freq[None, :]
    cos = jnp.cos(ang)[:, lane % half]
    sin = jnp.sin(ang)[:, lane % half]
    c = jnp.where(lane < ROPE_DIM, cos, 1.0)
    s1 = jnp.where(lane < half, -sin, 0.0)
    s2 = jnp.where((lane >= half) & (lane < ROPE_DIM), sin, 0.0)
    return jnp.stack([c, s1, s2]).astype(F32)


def kernel(x, norm_ffn1, ffn1_w_gate, ffn1_w_up, ffn1_w_down, norm_mix, w_in,
           w_proj_dil, w_proj_sb, w_out, norm_ffn2, ffn2_w_gate, ffn2_w_up,
           ffn2_w_down, norm_final):
    b, t, dm = x.shape
    depth = w_in.shape[0]
    assert w_in.shape[2] == 3 * D_DIL + 3 * N_SB_HEADS * HEAD_DIM + 2 * dm
    tab = _rope_table(t)
    gain3 = lambda g: g.reshape(-1, 1, dm)
    n1, nm, n2, fin = gain3(norm_ffn1), gain3(norm_mix), gain3(norm_ffn2), gain3(norm_final)

    x2 = x.reshape(b * t, dm)
    for l in range(depth):
        x2 = _ffn(x2, n1, ffn1_w_gate, ffn1_w_up, ffn1_w_down, fin, l, final=False)
        g0, g1, g2, sb = _qkv(x2.reshape(b, t, dm), nm, w_in, tab, l)
        o_dil = _dil_attention(g0, g1, g2)
        o_sb = _sb_attention(sb)
        x2 = _mix_out(x2, nm, o_dil.reshape(b * t, -1), o_sb.reshape(b * t, -1),
                      w_in, w_proj_dil, w_proj_sb, w_out, l)
        x2 = _ffn(x2, n2, ffn2_w_gate, ffn2_w_up, ffn2_w_down, fin, l,
                  final=(l == depth - 1))
    return x2.reshape(b, t, dm)
```
